```python
import math
import jax, jax.numpy as jnp
from jax import lax
import numpy as np

D_MODEL = 1024
BATCH = 4
SEQ = 4096
DEPTH = 1

CHUNK = 64
MIX_WIDTH = D_MODEL
GMLP_WIDTH = MIX_WIDTH // 2
DIFF_WIDTH = MIX_WIDTH - GMLP_WIDTH
GMLP_GROUPS = 4
GMLP_GROUP_CH = GMLP_WIDTH // GMLP_GROUPS
GMLP_BLOCK = 128
DIFF_HEADS = 4
DIFF_HEAD_DIM = DIFF_WIDTH // (2 * DIFF_HEADS)
DIFF_V_DIM = 2 * DIFF_HEAD_DIM
ATTN_QBLOCK = 128
A_COLS = 2 * GMLP_WIDTH
Q_COLS = DIFF_HEADS * 2 * DIFF_HEAD_DIM
K_COLS = DIFF_HEADS * 2 * DIFF_HEAD_DIM
V_COLS = DIFF_HEADS * DIFF_V_DIM
IN_COLS = A_COLS + Q_COLS + K_COLS + V_COLS
REL_BUCKETS = 32
REL_MAX_EXACT = 8
REL_MAX_DIST = 128
PEER_N_KEYS = 128
PEER_EXPERTS = PEER_N_KEYS * PEER_N_KEYS
PEER_HEADS = 8
PEER_TOPK = 16
PEER_DKEY = 256
PEER_HALF = PEER_DKEY // 2
PEER_BLOCK = 128
PLE_DIM = 256
EPS = 1e-6
NEG_INF = -1e30

kernel_name = "hybrid_gmlp_diffattn_peer_ple"


def _rmsnorm(x, g):
    xf = x.astype(jnp.float32)
    y = xf * lax.rsqrt(jnp.mean(xf * xf, axis=-1, keepdims=True) + EPS)
    return (y * g.astype(jnp.float32)).astype(x.dtype)


def _layernorm(x, g, b):
    xf = x.astype(jnp.float32)
    mu = jnp.mean(xf, axis=-1, keepdims=True)
    xc = xf - mu
    y = xc * lax.rsqrt(jnp.mean(xc * xc, axis=-1, keepdims=True) + EPS)
    return (y * g.astype(jnp.float32) + b.astype(jnp.float32)).astype(x.dtype)


def _t5_bucket(rel):
    nb = REL_BUCKETS // 2
    ret = jnp.where(rel > 0, nb, 0)
    n = jnp.abs(rel)
    is_small = n < REL_MAX_EXACT
    nf = jnp.maximum(n, 1).astype(jnp.float32)
    large = REL_MAX_EXACT + (jnp.log(nf / REL_MAX_EXACT) / math.log(REL_MAX_DIST / REL_MAX_EXACT)
                             * (nb - REL_MAX_EXACT)).astype(jnp.int32)
    large = jnp.minimum(large, nb - 1)
    return ret + jnp.where(is_small, n, large)


def _gmlp_group(z, ln_g, ln_b, w_s, b_s, beta):
    B, S, _ = z.shape
    z = jax.nn.gelu(z)
    u, v = z[..., :GMLP_WIDTH], z[..., GMLP_WIDTH:]
    v = v.reshape(B, S, GMLP_GROUPS, GMLP_GROUP_CH)
    v = _layernorm(v, ln_g.reshape(GMLP_GROUPS, GMLP_GROUP_CH), ln_b.reshape(GMLP_GROUPS, GMLP_GROUP_CH))
    v = v.reshape(B, S // GMLP_BLOCK, GMLP_BLOCK, GMLP_GROUPS, GMLP_GROUP_CH)
    pos = jnp.arange(GMLP_BLOCK)
    causal = (pos[None, :] // CHUNK) <= (pos[:, None] // CHUNK)
    w = jnp.where(causal[None], w_s, jnp.zeros_like(w_s))
    sv = jnp.einsum('gij,bnjgc->bnigc', w, v) + jnp.transpose(b_s)[None, None, :, :, None]
    out = u.reshape(B, S, GMLP_GROUPS, GMLP_GROUP_CH) * sv.reshape(B, S, GMLP_GROUPS, GMLP_GROUP_CH)
    out = _rmsnorm(out, beta.reshape(GMLP_GROUPS, GMLP_GROUP_CH))
    return out.reshape(B, S, GMLP_WIDTH)


def _diff_attention(q, k, v, rel_bias, lam, lam_init, subln_g):
    B, S = q.shape[0], q.shape[1]
    nblk = S // ATTN_QBLOCK
    qb = q.reshape(B, nblk, ATTN_QBLOCK, DIFF_HEADS, 2, DIFF_HEAD_DIM)
    qb = jnp.transpose(qb, (1, 0, 2, 3, 4, 5)) * (DIFF_HEAD_DIM ** -0.5)
    kpos = jnp.arange(S)

    def one_block(args):
        q_blk, bi = args
        qpos = bi * ATTN_QBLOCK + jnp.arange(ATTN_QBLOCK)
        logits = jnp.einsum('bqhcd,bkhcd->bhcqk', q_blk, k).astype(jnp.float32)
        bias = rel_bias[_t5_bucket(kpos[None, :] - qpos[:, None])]
        bias = jnp.transpose(bias, (2, 0, 1))[None, :, None].astype(jnp.float32)
        mask = (kpos[None, :] // CHUNK) <= (qpos[:, None] // CHUNK)
        logits = jnp.where(mask, logits + bias, NEG_INF)
        probs = jax.nn.softmax(logits, axis=-1)
        attn = probs[:, :, 0] - lam * probs[:, :, 1]
        return jnp.einsum('bhqk,bkhe->bqhe', attn.astype(v.dtype), v)

    out = lax.map(one_block, (qb, jnp.arange(nblk)))
    out = jnp.transpose(out, (1, 0, 2, 3, 4)).reshape(B, S, DIFF_HEADS, DIFF_V_DIM)
    out = _rmsnorm(out, subln_g) * (1.0 - lam_init)
    return out.reshape(B, S, DIFF_WIDTH)


def _peer(x, w_q, sub_keys, u_tab, v_tab):
    B, S, D = x.shape
    T = B * S
    xb = x.reshape(T // PEER_BLOCK, PEER_BLOCK, D)

    def one(xt):
        q = (xt @ w_q).reshape(PEER_BLOCK, PEER_HEADS, 2, PEER_HALF)
        s = jnp.einsum('thcd,hcnd->thcn', q, sub_keys).astype(jnp.float32)
        s1, i1 = lax.top_k(s[:, :, 0], PEER_TOPK)
        s2, i2 = lax.top_k(s[:, :, 1], PEER_TOPK)
        cand = (s1[..., :, None] + s2[..., None, :]).reshape(PEER_BLOCK, PEER_HEADS, PEER_TOPK * PEER_TOPK)
        cidx = (i1[..., :, None] * PEER_N_KEYS + i2[..., None, :]).reshape(PEER_BLOCK, PEER_HEADS, PEER_TOPK * PEER_TOPK)
        best, pos = lax.top_k(cand, PEER_TOPK)
        eid = jnp.take_along_axis(cidx, pos, axis=-1)
        gate = jax.nn.softmax(best, axis=-1)
        u_sel = u_tab[eid]
        act = jax.nn.gelu(jnp.einsum('thkd,td->thk', u_sel, xt).astype(jnp.float32))
        v_sel = v_tab[eid]
        return jnp.einsum('thk,thkd->td', (gate * act).astype(xt.dtype), v_sel)

    return lax.map(one, xb).reshape(B, S, D)


def setup_inputs(seed: int = 0) -> dict:
    key = jax.random.key(seed)
    ks = jax.random.split(key, 32)
    f32 = jnp.float32
    nrm = lambda k, shape, scale: jax.random.normal(k, shape, f32) * scale
    gain = lambda k, shape: 1.0 + 0.02 * jax.random.normal(k, shape, f32)
    return {
        'x': jax.random.normal(ks[0], (BATCH, SEQ, D_MODEL), f32),
        'p': jax.random.normal(ks[1], (DEPTH, BATCH, SEQ, PLE_DIM), f32),
        'g_mix': gain(ks[2], (DEPTH, D_MODEL)),
        'w_in': nrm(ks[3], (DEPTH, D_MODEL, IN_COLS), D_MODEL ** -0.5),
        'gmlp_ln_g': gain(ks[4], (DEPTH, GMLP_WIDTH)),
        'gmlp_ln_b': nrm(ks[5], (DEPTH, GMLP_WIDTH), 0.02),
        'gmlp_w_s': nrm(ks[6], (DEPTH, GMLP_GROUPS, GMLP_BLOCK, GMLP_BLOCK), GMLP_BLOCK ** -0.5),
        'gmlp_b_s': 1.0 + nrm(ks[7], (DEPTH, GMLP_GROUPS, GMLP_BLOCK), 0.02),
        'gmlp_beta': gain(ks[8], (DEPTH, GMLP_WIDTH)),
        'lambda_q1': nrm(ks[9], (DEPTH, DIFF_HEAD_DIM), 0.1),
        'lambda_k1': nrm(ks[10], (DEPTH, DIFF_HEAD_DIM), 0.1),
        'lambda_q2': nrm(ks[11], (DEPTH, DIFF_HEAD_DIM), 0.1),
        'lambda_k2': nrm(ks[12], (DEPTH, DIFF_HEAD_DIM), 0.1),
        'subln_g': gain(ks[13], (DEPTH, DIFF_V_DIM)),
        'rel_bias': nrm(ks[14], (REL_BUCKETS, DIFF_HEADS), 0.5),
        'w_out': nrm(ks[15], (DEPTH, MIX_WIDTH, D_MODEL), MIX_WIDTH ** -0.5),
        'g_ffn': gain(ks[16], (DEPTH, D_MODEL)),
        'peer_w_q': nrm(ks[17], (DEPTH, D_MODEL, PEER_HEADS * PEER_DKEY), D_MODEL ** -0.5),
        'peer_keys': nrm(ks[18], (DEPTH, PEER_HEADS, 2, PEER_N_KEYS, PEER_HALF), PEER_HALF ** -0.5),
        'peer_u': nrm(ks[19], (DEPTH, PEER_EXPERTS, D_MODEL), D_MODEL ** -0.5),
        'peer_v': nrm(ks[20], (DEPTH, PEER_EXPERTS, D_MODEL), PEER_HEADS ** -0.5),
        'g_ple': gain(ks[21], (DEPTH, D_MODEL)),
        'w_ple': nrm(ks[22], (DEPTH, PLE_DIM, D_MODEL), PLE_DIM ** -0.5),
        'w_gate': nrm(ks[23], (DEPTH, D_MODEL, D_MODEL), D_MODEL ** -0.5),
        'g_final': gain(ks[24], (D_MODEL,)),
    }


def reference(x, p, g_mix, w_in, gmlp_ln_g, gmlp_ln_b, gmlp_w_s, gmlp_b_s, gmlp_beta,
              lambda_q1, lambda_k1, lambda_q2, lambda_k2, subln_g, rel_bias, w_out,
              g_ffn, peer_w_q, peer_keys, peer_u, peer_v, g_ple, w_ple, w_gate, g_final):
    B, S, _ = x.shape
    h = x
    for i in range(DEPTH):
        lam_init = 0.8 - 0.6 * math.exp(-0.3 * i)
        n1 = _rmsnorm(h, g_mix[i])
        z = n1 @ w_in[i]
        z_a = z[..., :A_COLS]
        q = z[..., A_COLS:A_COLS + Q_COLS].reshape(B, S, DIFF_HEADS, 2, DIFF_HEAD_DIM)
        k = z[..., A_COLS + Q_COLS:A_COLS + Q_COLS + K_COLS].reshape(B, S, DIFF_HEADS, 2, DIFF_HEAD_DIM)
        v = z[..., A_COLS + Q_COLS + K_COLS:].reshape(B, S, DIFF_HEADS, DIFF_V_DIM)
        out_a = _gmlp_group(z_a, gmlp_ln_g[i], gmlp_ln_b[i], gmlp_w_s[i], gmlp_b_s[i], gmlp_beta[i])
        lam = (jnp.exp(jnp.sum(lambda_q1[i].astype(jnp.float32) * lambda_k1[i].astype(jnp.float32)))
               - jnp.exp(jnp.sum(lambda_q2[i].astype(jnp.float32) * lambda_k2[i].astype(jnp.float32)))
               + lam_init)
        out_b = _diff_attention(q, k, v, rel_bias, lam, lam_init, subln_g[i])
        h = h + jnp.concatenate([out_a, out_b], axis=-1) @ w_out[i]
        h = h + _peer(_rmsnorm(h, g_ffn[i]), peer_w_q[i], peer_keys[i], peer_u[i], peer_v[i])
        gate = jax.nn.sigmoid((_rmsnorm(h, g_ple[i]) @ w_gate[i]).astype(jnp.float32)).astype(h.dtype)
        h = h + (p[i] @ w_ple[i]) * gate
    return _rmsnorm(h, g_final)
```

```python
import functools
import math

import jax
import jax.numpy as jnp
from jax import lax
from jax.experimental import pallas as pl
from jax.experimental.pallas import tpu as pltpu

D_MODEL = 1024
CHUNK = 64
GMLP_WIDTH = 512
GMLP_GROUPS = 4
GMLP_GROUP_CH = 128
GMLP_BLOCK = 128
DIFF_HEADS = 4
DIFF_HEAD_DIM = 64
DIFF_V_DIM = 128
DIFF_WIDTH = 512
A_COLS = 1024
QKV_COLS = 1536
IN_COLS = A_COLS + QKV_COLS
REL_BUCKETS = 32
REL_MAX_EXACT = 8
PEER_N_KEYS = 128
PEER_HEADS = 8
PEER_TOPK = 16
PEER_HALF = 128
PEER_DKEY = 256
PLE_DIM = 256
EPS = 1e-6
NEG_INF = -1e30

V7X_VMEM_LIMIT_BYTES = 48 * 1024 * 1024

MIX_TM = 256
ATT_T = 256
ROUTE_TM = 256
PEER_TM = 8
PEER_ROWS = PEER_TM * PEER_HEADS * PEER_TOPK
PLE_TM = 512

_NT = (((1,), (1,)), ((), ()))
_BF = jnp.bfloat16
_F32 = jnp.float32


def _rms(x, g):
    return x * lax.rsqrt(jnp.mean(x * x, axis=-1, keepdims=True) + EPS) * g


def _mm(a, b):
    return jnp.dot(a.astype(_BF), b.astype(_BF), preferred_element_type=_F32)


def _mm_nt(a, b):
    return lax.dot_general(a.astype(_BF), b.astype(_BF), _NT, preferred_element_type=_F32)


def _mix_in_kernel(x_ref, g_ref, w_ref, lng_ref, lnb_ref, ws_ref, bst_ref, beta_ref,
                   outa_ref, qkv_ref):
    n1 = _rms(x_ref[...], g_ref[...])
    z = jnp.dot(n1.astype(_BF), w_ref[...], preferred_element_type=_F32)
    qkv_ref[...] = z[:, A_COLS:].astype(_BF)
    za = jax.nn.gelu(z[:, :A_COLS])
    pos_i = lax.broadcasted_iota(jnp.int32, (GMLP_BLOCK, GMLP_BLOCK), 0) // CHUNK
    pos_j = lax.broadcasted_iota(jnp.int32, (GMLP_BLOCK, GMLP_BLOCK), 1) // CHUNK
    causal = pos_j <= pos_i
    for g in range(GMLP_GROUPS):
        c0 = g * GMLP_GROUP_CH
        u = za[:, c0:c0 + GMLP_GROUP_CH]
        v = za[:, GMLP_WIDTH + c0:GMLP_WIDTH + c0 + GMLP_GROUP_CH]
        mu = jnp.mean(v, axis=-1, keepdims=True)
        vc = v - mu
        vn = vc * lax.rsqrt(jnp.mean(vc * vc, axis=-1, keepdims=True) + EPS)
        vn = vn * lng_ref[:, c0:c0 + GMLP_GROUP_CH] + lnb_ref[:, c0:c0 + GMLP_GROUP_CH]
        wm = jnp.where(causal, ws_ref[g], 0.0)
        for n in range(MIX_TM // GMLP_BLOCK):
            r0 = n * GMLP_BLOCK
            sv = _mm(wm, vn[r0:r0 + GMLP_BLOCK]) + bst_ref[:, g:g + 1]
            o = u[r0:r0 + GMLP_BLOCK] * sv
            outa_ref[r0:r0 + GMLP_BLOCK, c0:c0 + GMLP_GROUP_CH] = _rms(
                o, beta_ref[:, c0:c0 + GMLP_GROUP_CH])


def _mix_in(x2, g_mix, w_in_bf, ln_g, ln_b, w_s, b_s_t, beta):
    T = x2.shape[0]
    full = lambda shape: pl.BlockSpec(shape, lambda i: (0,) * len(shape))
    return pl.pallas_call(
        _mix_in_kernel,
        grid=(T // MIX_TM,),
        in_specs=[
            pl.BlockSpec((MIX_TM, D_MODEL), lambda i: (i, 0)),
            full((1, D_MODEL)),
            full((D_MODEL, IN_COLS)),
            full((1, GMLP_WIDTH)),
            full((1, GMLP_WIDTH)),
            full((GMLP_GROUPS, GMLP_BLOCK, GMLP_BLOCK)),
            full((GMLP_BLOCK, GMLP_GROUPS)),
            full((1, GMLP_WIDTH)),
        ],
        out_specs=[
            pl.BlockSpec((MIX_TM, GMLP_WIDTH), lambda i: (i, 0)),
            pl.BlockSpec((MIX_TM, QKV_COLS), lambda i: (i, 0)),
        ],
        out_shape=[
            jax.ShapeDtypeStruct((T, GMLP_WIDTH), _F32),
            jax.ShapeDtypeStruct((T, QKV_COLS), _BF),
        ],
        compiler_params=pltpu.CompilerParams(
            dimension_semantics=("parallel",), vmem_limit_bytes=V7X_VMEM_LIMIT_BYTES),
        name="mix_in",
    )(x2, g_mix, w_in_bf, ln_g, ln_b, w_s, b_s_t, beta)


def _relbias_kernel(rb_ref, out_ref):
    qi = lax.broadcasted_iota(jnp.int32, (ATT_T, ATT_T), 0)
    kj = lax.broadcasted_iota(jnp.int32, (ATT_T, ATT_T), 1)
    for d in range(2):
        rel = kj - qi - d * ATT_T
        n = jnp.abs(rel)
        n2 = n * n
        large = jnp.full_like(n, REL_MAX_EXACT)
        for k in range(1, 8):
            large = large + (n2 >= (1 << (6 + k))).astype(jnp.int32)
        bucket = jnp.where(rel > 0, REL_BUCKETS // 2, 0) + jnp.where(n < REL_MAX_EXACT, n, large)
        for h in range(DIFF_HEADS):
            bias = jnp.zeros((ATT_T, ATT_T), _F32)
            for b in range(REL_BUCKETS):
                bias = jnp.where(bucket == b, rb_ref[b, h], bias)
            if d == 0:
                bias = jnp.where((kj // CHUNK) <= (qi // CHUNK), bias, NEG_INF)
            out_ref[h, d] = bias


def _relbias(rel_bias):
    return pl.pallas_call(
        _relbias_kernel,
        in_specs=[pl.BlockSpec(memory_space=pltpu.SMEM)],
        out_specs=pl.BlockSpec(memory_space=pltpu.VMEM),
        out_shape=jax.ShapeDtypeStruct((DIFF_HEADS, 2, ATT_T, ATT_T), _F32),
        name="relbias",
    )(rel_bias)


def _diffattn_kernel(lam_init, rb_ref, q_ref, k_ref, v_ref, bias_ref, lq1_ref, lk1_ref,
                     lq2_ref, lk2_ref, sg_ref, out_ref):
    h = pl.program_id(1)
    qi = pl.program_id(2)
    q = q_ref[...] * (DIFF_HEAD_DIM ** -0.5)
    lane = lax.broadcasted_iota(jnp.int32, q.shape, 1)
    zero = jnp.zeros_like(q)
    qs = (jnp.where(lane < DIFF_HEAD_DIM, q, zero), jnp.where(lane >= DIFF_HEAD_DIM, q, zero))

    def tile(j):
        r0 = pl.multiple_of(j * ATT_T, ATT_T)
        return k_ref[pl.ds(r0, ATT_T), :], v_ref[pl.ds(r0, ATT_T), :]

    kt, vt = tile(qi)
    carry = []
    for c in range(2):
        s = lax.dot_general(qs[c], kt, _NT, preferred_element_type=_F32) + bias_ref[0]
        m = jnp.max(s, axis=-1, keepdims=True)
        p = jnp.exp(s - m)
        carry += [m, jnp.sum(p, axis=-1, keepdims=True),
                  jnp.dot(p.astype(_BF), vt, preferred_element_type=_F32)]

    far_bias = rb_ref[REL_BUCKETS // 2 - 1, h]

    def body(j, carry):
        kt, vt = tile(j)
        bias = jnp.where(j == qi - 1, bias_ref[1], far_bias)
        new = []
        for c in range(2):
            m, l, acc = carry[3 * c:3 * c + 3]
            s = lax.dot_general(qs[c], kt, _NT, preferred_element_type=_F32) + bias
            m_new = jnp.maximum(m, jnp.max(s, axis=-1, keepdims=True))
            a = jnp.exp(m - m_new)
            p = jnp.exp(s - m_new)
            new += [m_new, a * l + jnp.sum(p, axis=-1, keepdims=True),
                    a * acc + jnp.dot(p.astype(_BF), vt, preferred_element_type=_F32)]
        return tuple(new)

    m1, l1, a1, m2, l2, a2 = lax.fori_loop(0, qi, body, tuple(carry))
    lam = (jnp.exp(jnp.sum(lq1_ref[...] * lk1_ref[...], axis=-1, keepdims=True))
           - jnp.exp(jnp.sum(lq2_ref[...] * lk2_ref[...], axis=-1, keepdims=True)) + lam_init)
    o = a1 / l1 - lam * (a2 / l2)
    out_ref[...] = _rms(o, sg_ref[...]) * (1.0 - lam_init)


def _diffattn(qkv3, bias_tiles, rel_bias, lq1, lk1, lq2, lk2, subln_g, lam_init):
    B, S, _ = qkv3.shape
    nq = S // ATT_T
    vec = lambda n: pl.BlockSpec((1, n), lambda b, h, i: (0, 0))
    return pl.pallas_call(
        functools.partial(_diffattn_kernel, lam_init),
        grid=(B, DIFF_HEADS, nq),
        in_specs=[
            pl.BlockSpec(memory_space=pltpu.SMEM),
            pl.BlockSpec((None, ATT_T, 128), lambda b, h, i: (b, i, h)),
            pl.BlockSpec((None, S, 128), lambda b, h, i: (b, 0, DIFF_HEADS + h)),
            pl.BlockSpec((None, S, 128), lambda b, h, i: (b, 0, 2 * DIFF_HEADS + h)),
            pl.BlockSpec((None, 2, ATT_T, ATT_T), lambda b, h, i: (h, 0, 0, 0)),
            vec(DIFF_HEAD_DIM), vec(DIFF_HEAD_DIM), vec(DIFF_HEAD_DIM), vec(DIFF_HEAD_DIM),
            vec(DIFF_V_DIM),
        ],
        out_specs=pl.BlockSpec((None, ATT_T, DIFF_V_DIM), lambda b, h, i: (b, i, h)),
        out_shape=jax.ShapeDtypeStruct((B, S, DIFF_WIDTH), _F32),
        compiler_params=pltpu.CompilerParams(
            dimension_semantics=("parallel", "parallel", "arbitrary"),
            vmem_limit_bytes=V7X_VMEM_LIMIT_BYTES),
        name="diffattn",
    )(rel_bias, qkv3, qkv3, qkv3, bias_tiles, lq1, lk1, lq2, lk2, subln_g)


def _top16(s, fill):
    n_rows = s.shape[0]
    iota = lax.broadcasted_iota(jnp.int32, s.shape, 0)
    vals, poss = [], []
    for _ in range(PEER_TOPK):
        m = jnp.max(s, axis=0, keepdims=True)
        pos = jnp.min(jnp.where(s == m, iota, n_rows), axis=0, keepdims=True)
        vals.append(m)
        poss.append(pos)
        s = jnp.where(iota == pos, fill, s)
    return vals, poss


def _route_kernel(x_ref, oa_ref, ob_ref, wo_ref, g_ref, wq_ref, keys_ref,
                  h1_ref, eid_ref, gate_ref, qp_scr, eid_scr, gate_scr):
    mix = jnp.concatenate([oa_ref[...], ob_ref[...]], axis=-1)
    h1 = x_ref[...] + _mm(mix, wo_ref[...])
    h1_ref[...] = h1
    xn = _rms(h1, g_ref[...])
    qp_scr[...] = _mm(xn, wq_ref[...])
    neg = -jnp.inf

    def head(h, _):
        c0 = pl.multiple_of(h * PEER_DKEY, PEER_DKEY)
        q1 = qp_scr[:, pl.ds(c0, PEER_HALF)]
        q2 = qp_scr[:, pl.ds(c0 + PEER_HALF, PEER_HALF)]
        s1 = _mm_nt(keys_ref[h, 0], q1)
        s2 = _mm_nt(keys_ref[h, 1], q2)
        v1, i1 = _top16(s1, neg)
        v2, i2 = _top16(s2, neg)
        v2c = jnp.concatenate(v2, axis=0)
        i2c = jnp.concatenate(i2, axis=0)
        cand = jnp.concatenate([a + v2c for a in v1], axis=0)
        cidx = jnp.concatenate([a * PEER_N_KEYS + i2c for a in i1], axis=0)
        best, pos = _top16(cand, neg)
        iota = lax.broadcasted_iota(jnp.int32, cand.shape, 0)
        eid = jnp.concatenate(
            [jnp.max(jnp.where(iota == p, cidx, -1), axis=0, keepdims=True) for p in pos], axis=0)
        best = jnp.concatenate(best, axis=0)
        e = jnp.exp(best - best[0:1])
        gate = e / jnp.sum(e, axis=0, keepdims=True)
        r0 = pl.multiple_of(h * PEER_TOPK, PEER_TOPK)
        eid_scr[pl.ds(r0, PEER_TOPK), :] = eid
        gate_scr[pl.ds(r0, PEER_TOPK), :] = gate
        return 0

    lax.fori_loop(0, PEER_HEADS, head, 0)
    eid_ref[...] = eid_scr[...].T
    gate_ref[...] = gate_scr[...].T


def _route(x2, out_a, out_b, w_out_bf, g_ffn, w_q_bf, keys_bf):
    T = x2.shape[0]
    tm = ROUTE_TM
    hk = PEER_HEADS * PEER_TOPK
    full = lambda shape: pl.BlockSpec(shape, lambda i: (0,) * len(shape))
    return pl.pallas_call(
        _route_kernel,
        grid=(T // tm,),
        in_specs=[
            pl.BlockSpec((tm, D_MODEL), lambda i: (i, 0)),
            pl.BlockSpec((tm, GMLP_WIDTH), lambda i: (i, 0)),
            pl.BlockSpec((tm, DIFF_WIDTH), lambda i: (i, 0)),
            full((D_MODEL, D_MODEL)),
            full((1, D_MODEL)),
            full((D_MODEL, PEER_HEADS * PEER_DKEY)),
            full((PEER_HEADS, 2, PEER_N_KEYS, PEER_HALF)),
        ],
        out_specs=[
            pl.BlockSpec((tm, D_MODEL), lambda i: (i, 0)),
            pl.BlockSpec((tm, hk), lambda i: (i, 0)),
            pl.BlockSpec((tm, hk), lambda i: (i, 0)),
        ],
        out_shape=[
            jax.ShapeDtypeStruct((T, D_MODEL), _F32),
            jax.ShapeDtypeStruct((T, hk), jnp.int32),
            jax.ShapeDtypeStruct((T, hk), _F32),
        ],
        scratch_shapes=[
            pltpu.VMEM((tm, PEER_HEADS * PEER_DKEY), _F32),
            pltpu.VMEM((hk, tm), jnp.int32),
            pltpu.VMEM((hk, tm), _F32),
        ],
        compiler_params=pltpu.CompilerParams(
            dimension_semantics=("parallel",), vmem_limit_bytes=V7X_VMEM_LIMIT_BYTES),
        name="route",
    )(x2, out_a, out_b, w_out_bf, g_ffn, w_q_bf, keys_bf)


def _peer_kernel(eid0_ref, eidn_ref, h1_ref, gate_ref, g_ref, u_hbm, v_hbm, h2_ref,
                 ubuf, vbuf, sem):
    i = pl.program_id(0)
    n = pl.num_programs(0)
    slot = i % 2

    def row_copies(eid_ref, r, s):
        e = eid_ref[0, 0, r]
        return (pltpu.make_async_copy(u_hbm.at[pl.ds(e, 1)], ubuf.at[s, pl.ds(r, 1)], sem.at[0, s]),
                pltpu.make_async_copy(v_hbm.at[pl.ds(e, 1)], vbuf.at[s, pl.ds(r, 1)], sem.at[1, s]))

    def issue(eid_ref, s):
        def body(r8, _):
            for k in range(8):
                cu, cv = row_copies(eid_ref, r8 * 8 + k, s)
                cu.start()
                cv.start()
            return 0
        lax.fori_loop(0, PEER_ROWS // 8, body, 0)

    @pl.when(i == 0)
    def _():
        issue(eid0_ref, 0)

    @pl.when(i + 1 < n)
    def _():
        issue(eidn_ref, 1 - slot)

    pltpu.make_async_copy(u_hbm.at[pl.ds(0, PEER_ROWS)], ubuf.at[slot], sem.at[0, slot]).wait()
    pltpu.make_async_copy(v_hbm.at[pl.ds(0, PEER_ROWS)], vbuf.at[slot], sem.at[1, slot]).wait()

    h1 = h1_ref[...]
    xn = _rms(h1, g_ref[...])
    act = _mm_nt(xn, ubuf[slot])
    own = (lax.broadcasted_iota(jnp.int32, act.shape, 1) // (PEER_HEADS * PEER_TOPK)
           == lax.broadcasted_iota(jnp.int32, act.shape, 0))
    gate = jnp.concatenate([gate_ref[...]] * PEER_TM, axis=-1)
    w = jnp.where(own, gate * jax.nn.gelu(act), 0.0)
    h2_ref[...] = h1 + _mm(w, vbuf[slot])


def _peer(eid3, h1, gate, g_ffn, peer_u, peer_v):
    T = h1.shape[0]
    n = T // PEER_TM
    hk = PEER_HEADS * PEER_TOPK
    return pl.pallas_call(
        _peer_kernel,
        grid=(n,),
        in_specs=[
            pl.BlockSpec((1, 1, PEER_ROWS), lambda i: (0, 0, 0), memory_space=pltpu.SMEM),
            pl.BlockSpec((1, 1, PEER_ROWS), lambda i: (jnp.minimum(i + 1, n - 1), 0, 0),
                         memory_space=pltpu.SMEM),
            pl.BlockSpec((PEER_TM, D_MODEL), lambda i: (i, 0)),
            pl.BlockSpec((PEER_TM, hk), lambda i: (i, 0)),
            pl.BlockSpec((1, D_MODEL), lambda i: (0, 0)),
            pl.BlockSpec(memory_space=pl.ANY),
            pl.BlockSpec(memory_space=pl.ANY),
        ],
        out_specs=pl.BlockSpec((PEER_TM, D_MODEL), lambda i: (i, 0)),
        out_shape=jax.ShapeDtypeStruct((T, D_MODEL), _F32),
        scratch_shapes=[
            pltpu.VMEM((2, PEER_ROWS, D_MODEL), _F32),
            pltpu.VMEM((2, PEER_ROWS, D_MODEL), _F32),
            pltpu.SemaphoreType.DMA((2, 2)),
        ],
        compiler_params=pltpu.CompilerParams(
            dimension_semantics=("arbitrary",), vmem_limit_bytes=V7X_VMEM_LIMIT_BYTES),
        name="peer",
    )(eid3, eid3, h1, gate, g_ffn, peer_u, peer_v)


def _ple_kernel(final, h_ref, p_ref, g_ref, wg_ref, wp_ref, gf_ref, out_ref):
    h = h_ref[...]
    gate = jax.nn.sigmoid(_mm(_rms(h, g_ref[...]), wg_ref[...]))
    h = h + _mm(p_ref[...], wp_ref[...]) * gate
    out_ref[...] = _rms(h, gf_ref[...]) if final else h


def _ple(h2, p2, g_ple, w_gate_bf, w_ple_bf, g_final, final):
    T = h2.shape[0]
    full = lambda shape: pl.BlockSpec(shape, lambda i: (0,) * len(shape))
    return pl.pallas_call(
        functools.partial(_ple_kernel, final),
        grid=(T // PLE_TM,),
        in_specs=[
            pl.BlockSpec((PLE_TM, D_MODEL), lambda i: (i, 0)),
            pl.BlockSpec((PLE_TM, PLE_DIM), lambda i: (i, 0)),
            full((1, D_MODEL)),
            full((D_MODEL, D_MODEL)),
            full((PLE_DIM, D_MODEL)),
            full((1, D_MODEL)),
        ],
        out_specs=pl.BlockSpec((PLE_TM, D_MODEL), lambda i: (i, 0)),
        out_shape=jax.ShapeDtypeStruct((T, D_MODEL), _F32),
        compiler_params=pltpu.CompilerParams(
            dimension_semantics=("parallel",), vmem_limit_bytes=V7X_VMEM_LIMIT_BYTES),
        name="ple",
    )(h2, p2, g_ple, w_gate_bf, w_ple_bf, g_final)


def kernel(x, p, g_mix, w_in, gmlp_ln_g, gmlp_ln_b, gmlp_w_s, gmlp_b_s, gmlp_beta, lambda_q1,
           lambda_k1, lambda_q2, lambda_k2, subln_g, rel_bias, w_out, g_ffn, peer_w_q, peer_keys,
           peer_u, peer_v, g_ple, w_ple, w_gate, g_final):
    B, S, D = x.shape
    depth = w_in.shape[0]
    T = B * S
    row = lambda a: a.reshape(1, -1)
    h = x.reshape(T, D)
    bias_tiles = _relbias(rel_bias)
    for i in range(depth):
        lam_init = 0.8 - 0.6 * math.exp(-0.3 * i)
        out_a, qkv = _mix_in(h, row(g_mix[i]), w_in[i].astype(_BF), row(gmlp_ln_g[i]),
                             row(gmlp_ln_b[i]), gmlp_w_s[i], jnp.transpose(gmlp_b_s[i]),
                             row(gmlp_beta[i]))
        out_b = _diffattn(qkv.reshape(B, S, QKV_COLS), bias_tiles, rel_bias, row(lambda_q1[i]),
                          row(lambda_k1[i]), row(lambda_q2[i]), row(lambda_k2[i]),
                          row(subln_g[i]), lam_init)
        h1, eid, gate = _route(h, out_a, out_b.reshape(T, DIFF_WIDTH), w_out[i].astype(_BF),
                               row(g_ffn[i]), peer_w_q[i].astype(_BF), peer_keys[i].astype(_BF))
        h2 = _peer(eid.reshape(T // PEER_TM, 1, PEER_ROWS), h1, gate, row(g_ffn[i]),
                   peer_u[i], peer_v[i])
        h = _ple(h2, p[i].reshape(T, PLE_DIM), row(g_ple[i]), w_gate[i].astype(_BF),
                 w_ple[i].astype(_BF), row(g_final), i == depth - 1)
    return h.reshape(B, S, D)
```

```python
import functools
import math

import jax
import jax.numpy as jnp
from jax import lax
from jax.experimental import pallas as pl
from jax.experimental.pallas import tpu as pltpu

D_MODEL = 1024
CHUNK = 64
GMLP_WIDTH = 512
GMLP_GROUPS = 4
GMLP_GROUP_CH = 128
GMLP_BLOCK = 128
DIFF_HEADS = 4
DIFF_HEAD_DIM = 64
DIFF_V_DIM = 128
DIFF_WIDTH = 512
A_COLS = 1024
QKV_COLS = 1536
IN_COLS = A_COLS + QKV_COLS
REL_BUCKETS = 32
REL_MAX_EXACT = 8
PEER_N_KEYS = 128
PEER_HEADS = 8
PEER_TOPK = 16
PEER_HALF = 128
PEER_DKEY = 256
PLE_DIM = 256
EPS = 1e-6
NEG_INF = -1e30

V7X_VMEM_LIMIT_BYTES = 48 * 1024 * 1024

MIX_TM = 256
ATT_T = 256
ROUTE_TM = 256
PEER_TM = 256
PEER_I1_TILE = 8
PLE_TM = 512

_NT = (((1,), (1,)), ((), ()))
_BF = jnp.bfloat16
_F32 = jnp.float32


def _rms(x, g):
    return x * lax.rsqrt(jnp.mean(x * x, axis=-1, keepdims=True) + EPS) * g


def _mm(a, b):
    return jnp.dot(a.astype(_BF), b.astype(_BF), preferred_element_type=_F32)


def _mm_nt(a, b):
    return lax.dot_general(a.astype(_BF), b.astype(_BF), _NT, preferred_element_type=_F32)


def _mix_in_kernel(x_ref, g_ref, w_ref, lng_ref, lnb_ref, ws_ref, bst_ref, beta_ref,
                   outa_ref, qkv_ref):
    n1 = _rms(x_ref[...], g_ref[...])
    z = jnp.dot(n1.astype(_BF), w_ref[...], preferred_element_type=_F32)
    qkv_ref[...] = z[:, A_COLS:].astype(_BF)
    za = jax.nn.gelu(z[:, :A_COLS])
    pos_i = lax.broadcasted_iota(jnp.int32, (GMLP_BLOCK, GMLP_BLOCK), 0) // CHUNK
    pos_j = lax.broadcasted_iota(jnp.int32, (GMLP_BLOCK, GMLP_BLOCK), 1) // CHUNK
    causal = pos_j <= pos_i
    for g in range(GMLP_GROUPS):
        c0 = g * GMLP_GROUP_CH
        u = za[:, c0:c0 + GMLP_GROUP_CH]
        v = za[:, GMLP_WIDTH + c0:GMLP_WIDTH + c0 + GMLP_GROUP_CH]
        mu = jnp.mean(v, axis=-1, keepdims=True)
        vc = v - mu
        vn = vc * lax.rsqrt(jnp.mean(vc * vc, axis=-1, keepdims=True) + EPS)
        vn = vn * lng_ref[:, c0:c0 + GMLP_GROUP_CH] + lnb_ref[:, c0:c0 + GMLP_GROUP_CH]
        wm = jnp.where(causal, ws_ref[g], 0.0)
        for n in range(MIX_TM // GMLP_BLOCK):
            r0 = n * GMLP_BLOCK
            sv = _mm(wm, vn[r0:r0 + GMLP_BLOCK]) + bst_ref[:, g:g + 1]
            o = u[r0:r0 + GMLP_BLOCK] * sv
            outa_ref[r0:r0 + GMLP_BLOCK, c0:c0 + GMLP_GROUP_CH] = _rms(
                o, beta_ref[:, c0:c0 + GMLP_GROUP_CH])


def _mix_in(x2, g_mix, w_in_bf, ln_g, ln_b, w_s, b_s_t, beta):
    T = x2.shape[0]
    full = lambda shape: pl.BlockSpec(shape, lambda i: (0,) * len(shape))
    return pl.pallas_call(
        _mix_in_kernel,
        grid=(T // MIX_TM,),
        in_specs=[
            pl.BlockSpec((MIX_TM, D_MODEL), lambda i: (i, 0)),
            full((1, D_MODEL)),
            full((D_MODEL, IN_COLS)),
            full((1, GMLP_WIDTH)),
            full((1, GMLP_WIDTH)),
            full((GMLP_GROUPS, GMLP_BLOCK, GMLP_BLOCK)),
            full((GMLP_BLOCK, GMLP_GROUPS)),
            full((1, GMLP_WIDTH)),
        ],
        out_specs=[
            pl.BlockSpec((MIX_TM, GMLP_WIDTH), lambda i: (i, 0)),
            pl.BlockSpec((MIX_TM, QKV_COLS), lambda i: (i, 0)),
        ],
        out_shape=[
            jax.ShapeDtypeStruct((T, GMLP_WIDTH), _F32),
            jax.ShapeDtypeStruct((T, QKV_COLS), _BF),
        ],
        compiler_params=pltpu.CompilerParams(
            dimension_semantics=("parallel",), vmem_limit_bytes=V7X_VMEM_LIMIT_BYTES),
        name="mix_in",
    )(x2, g_mix, w_in_bf, ln_g, ln_b, w_s, b_s_t, beta)


def _relbias_kernel(rb_ref, out_ref):
    qi = lax.broadcasted_iota(jnp.int32, (ATT_T, ATT_T), 0)
    kj = lax.broadcasted_iota(jnp.int32, (ATT_T, ATT_T), 1)
    for d in range(2):
        rel = kj - qi - d * ATT_T
        n = jnp.abs(rel)
        n2 = n * n
        large = jnp.full_like(n, REL_MAX_EXACT)
        for k in range(1, 8):
            large = large + (n2 >= (1 << (6 + k))).astype(jnp.int32)
        bucket = jnp.where(rel > 0, REL_BUCKETS // 2, 0) + jnp.where(n < REL_MAX_EXACT, n, large)
        for h in range(DIFF_HEADS):
            bias = jnp.zeros((ATT_T, ATT_T), _F32)
            for b in range(REL_BUCKETS):
                bias = jnp.where(bucket == b, rb_ref[b, h], bias)
            if d == 0:
                bias = jnp.where((kj // CHUNK) <= (qi // CHUNK), bias, NEG_INF)
            out_ref[h, d] = bias


def _relbias(rel_bias):
    return pl.pallas_call(
        _relbias_kernel,
        in_specs=[pl.BlockSpec(memory_space=pltpu.SMEM)],
        out_specs=pl.BlockSpec(memory_space=pltpu.VMEM),
        out_shape=jax.ShapeDtypeStruct((DIFF_HEADS, 2, ATT_T, ATT_T), _F32),
        name="relbias",
    )(rel_bias)


def _diffattn_kernel(lam_init, rb_ref, q_ref, k_ref, v_ref, bias_ref, lq1_ref, lk1_ref,
                     lq2_ref, lk2_ref, sg_ref, out_ref):
    h = pl.program_id(1)
    qi = pl.program_id(2)
    q = q_ref[...] * (DIFF_HEAD_DIM ** -0.5)
    lane = lax.broadcasted_iota(jnp.int32, q.shape, 1)
    zero = jnp.zeros_like(q)
    qs = (jnp.where(lane < DIFF_HEAD_DIM, q, zero), jnp.where(lane >= DIFF_HEAD_DIM, q, zero))

    def tile(j):
        r0 = pl.multiple_of(j * ATT_T, ATT_T)
        return k_ref[pl.ds(r0, ATT_T), :], v_ref[pl.ds(r0, ATT_T), :]

    kt, vt = tile(qi)
    carry = []
    for c in range(2):
        s = lax.dot_general(qs[c], kt, _NT, preferred_element_type=_F32) + bias_ref[0]
        m = jnp.max(s, axis=-1, keepdims=True)
        p = jnp.exp(s - m)
        carry += [m, jnp.sum(p, axis=-1, keepdims=True),
                  jnp.dot(p.astype(_BF), vt, preferred_element_type=_F32)]

    far_bias = rb_ref[REL_BUCKETS // 2 - 1, h]

    def body(j, carry):
        kt, vt = tile(j)
        bias = jnp.where(j == qi - 1, bias_ref[1], far_bias)
        new = []
        for c in range(2):
            m, l, acc = carry[3 * c:3 * c + 3]
            s = lax.dot_general(qs[c], kt, _NT, preferred_element_type=_F32) + bias
            m_new = jnp.maximum(m, jnp.max(s, axis=-1, keepdims=True))
            a = jnp.exp(m - m_new)
            p = jnp.exp(s - m_new)
            new += [m_new, a * l + jnp.sum(p, axis=-1, keepdims=True),
                    a * acc + jnp.dot(p.astype(_BF), vt, preferred_element_type=_F32)]
        return tuple(new)

    m1, l1, a1, m2, l2, a2 = lax.fori_loop(0, qi, body, tuple(carry))
    lam = (jnp.exp(jnp.sum(lq1_ref[...] * lk1_ref[...], axis=-1, keepdims=True))
           - jnp.exp(jnp.sum(lq2_ref[...] * lk2_ref[...], axis=-1, keepdims=True)) + lam_init)
    o = a1 / l1 - lam * (a2 / l2)
    out_ref[...] = _rms(o, sg_ref[...]) * (1.0 - lam_init)


def _diffattn(qkv3, bias_tiles, rel_bias, lq1, lk1, lq2, lk2, subln_g, lam_init):
    B, S, _ = qkv3.shape
    nq = S // ATT_T
    vec = lambda n: pl.BlockSpec((1, n), lambda b, h, i: (0, 0))
    return pl.pallas_call(
        functools.partial(_diffattn_kernel, lam_init),
        grid=(B, DIFF_HEADS, nq),
        in_specs=[
            pl.BlockSpec(memory_space=pltpu.SMEM),
            pl.BlockSpec((None, ATT_T, 128), lambda b, h, i: (b, i, h)),
            pl.BlockSpec((None, S, 128), lambda b, h, i: (b, 0, DIFF_HEADS + h)),
            pl.BlockSpec((None, S, 128), lambda b, h, i: (b, 0, 2 * DIFF_HEADS + h)),
            pl.BlockSpec((None, 2, ATT_T, ATT_T), lambda b, h, i: (h, 0, 0, 0)),
            vec(DIFF_HEAD_DIM), vec(DIFF_HEAD_DIM), vec(DIFF_HEAD_DIM), vec(DIFF_HEAD_DIM),
            vec(DIFF_V_DIM),
        ],
        out_specs=pl.BlockSpec((None, ATT_T, DIFF_V_DIM), lambda b, h, i: (b, i, h)),
        out_shape=jax.ShapeDtypeStruct((B, S, DIFF_WIDTH), _F32),
        compiler_params=pltpu.CompilerParams(
            dimension_semantics=("parallel", "parallel", "arbitrary"),
            vmem_limit_bytes=V7X_VMEM_LIMIT_BYTES),
        name="diffattn",
    )(rel_bias, qkv3, qkv3, qkv3, bias_tiles, lq1, lk1, lq2, lk2, subln_g)


def _top16(s, fill):
    n_rows = s.shape[0]
    iota = lax.broadcasted_iota(jnp.int32, s.shape, 0)
    vals, poss = [], []
    for _ in range(PEER_TOPK):
        m = jnp.max(s, axis=0, keepdims=True)
        pos = jnp.min(jnp.where(s == m, iota, n_rows), axis=0, keepdims=True)
        vals.append(m)
        poss.append(pos)
        s = jnp.where(iota == pos, fill, s)
    return vals, poss


def _route_kernel(x_ref, oa_ref, ob_ref, wo_ref, g_ref, wq_ref, keys_ref,
                  h1_ref, eid_ref, gate_ref, qp_scr, eid_scr, gate_scr):
    mix = jnp.concatenate([oa_ref[...], ob_ref[...]], axis=-1)
    h1 = x_ref[...] + _mm(mix, wo_ref[...])
    h1_ref[...] = h1
    xn = _rms(h1, g_ref[...])
    qp_scr[...] = _mm(xn, wq_ref[...])
    neg = -jnp.inf

    def head(h, _):
        c0 = pl.multiple_of(h * PEER_DKEY, PEER_DKEY)
        q1 = qp_scr[:, pl.ds(c0, PEER_HALF)]
        q2 = qp_scr[:, pl.ds(c0 + PEER_HALF, PEER_HALF)]
        s1 = _mm_nt(keys_ref[h, 0], q1)
        s2 = _mm_nt(keys_ref[h, 1], q2)
        v1, i1 = _top16(s1, neg)
        v2, i2 = _top16(s2, neg)
        v2c = jnp.concatenate(v2, axis=0)
        i2c = jnp.concatenate(i2, axis=0)
        cand = jnp.concatenate([a + v2c for a in v1], axis=0)
        cidx = jnp.concatenate([a * PEER_N_KEYS + i2c for a in i1], axis=0)
        best, pos = _top16(cand, neg)
        iota = lax.broadcasted_iota(jnp.int32, cand.shape, 0)
        eid = jnp.concatenate(
            [jnp.max(jnp.where(iota == p, cidx, -1), axis=0, keepdims=True) for p in pos], axis=0)
        best = jnp.concatenate(best, axis=0)
        e = jnp.exp(best - best[0:1])
        gate = e / jnp.sum(e, axis=0, keepdims=True)
        r0 = pl.multiple_of(h * PEER_TOPK, PEER_TOPK)
        eid_scr[pl.ds(r0, PEER_TOPK), :] = eid
        gate_scr[pl.ds(r0, PEER_TOPK), :] = gate
        return 0

    lax.fori_loop(0, PEER_HEADS, head, 0)
    eid_ref[...] = eid_scr[...].T
    gate_ref[...] = gate_scr[...].T


def _route(x2, out_a, out_b, w_out_bf, g_ffn, w_q_bf, keys_bf):
    T = x2.shape[0]
    tm = ROUTE_TM
    hk = PEER_HEADS * PEER_TOPK
    full = lambda shape: pl.BlockSpec(shape, lambda i: (0,) * len(shape))
    return pl.pallas_call(
        _route_kernel,
        grid=(T // tm,),
        in_specs=[
            pl.BlockSpec((tm, D_MODEL), lambda i: (i, 0)),
            pl.BlockSpec((tm, GMLP_WIDTH), lambda i: (i, 0)),
            pl.BlockSpec((tm, DIFF_WIDTH), lambda i: (i, 0)),
            full((D_MODEL, D_MODEL)),
            full((1, D_MODEL)),
            full((D_MODEL, PEER_HEADS * PEER_DKEY)),
            full((PEER_HEADS, 2, PEER_N_KEYS, PEER_HALF)),
        ],
        out_specs=[
            pl.BlockSpec((tm, D_MODEL), lambda i: (i, 0)),
            pl.BlockSpec((tm, hk), lambda i: (i, 0)),
            pl.BlockSpec((tm, hk), lambda i: (i, 0)),
        ],
        out_shape=[
            jax.ShapeDtypeStruct((T, D_MODEL), _F32),
            jax.ShapeDtypeStruct((T, hk), jnp.int32),
            jax.ShapeDtypeStruct((T, hk), _F32),
        ],
        scratch_shapes=[
            pltpu.VMEM((tm, PEER_HEADS * PEER_DKEY), _F32),
            pltpu.VMEM((hk, tm), jnp.int32),
            pltpu.VMEM((hk, tm), _F32),
        ],
        compiler_params=pltpu.CompilerParams(
            dimension_semantics=("parallel",), vmem_limit_bytes=V7X_VMEM_LIMIT_BYTES),
        name="route",
    )(x2, out_a, out_b, w_out_bf, g_ffn, w_q_bf, keys_bf)


def _peer_kernel(eid_ref, gate_ref, h1_ref, g_ref, ut_ref, v_ref, h2_ref, w_scr, xn_scr, acc_scr):
    j = pl.program_id(1)
    sub = 8

    @pl.when(j == 0)
    def _():
        xn_scr[...] = _rms(h1_ref[...], g_ref[...]).astype(_BF)
        acc_scr[...] = jnp.zeros_like(acc_scr)
        iota0 = lax.broadcasted_iota(jnp.int32, (PEER_N_KEYS, PEER_HEADS * PEER_TOPK), 0)

        def group(tg, _):
            for tl in range(sub):
                t = tg * sub + tl
                e_row = eid_ref[pl.ds(t, 1), :]
                pt = jnp.where(iota0 == (e_row >> 7), gate_ref[pl.ds(t, 1), :], 0.0)
                qt = jnp.where(iota0 == (e_row & (PEER_N_KEYS - 1)), 1.0, 0.0)
                w_scr[tg, pl.ds(tl, PEER_N_KEYS, stride=sub), :] = _mm_nt(pt, qt)
            return 0

        lax.fori_loop(0, PEER_TM // sub, group, 0)

    a = jnp.dot(xn_scr[...], ut_ref[...], preferred_element_type=_F32)
    wa = []
    for il in range(PEER_I1_TILE):
        r0 = pl.multiple_of((j * PEER_I1_TILE + il) * sub, sub)
        w = w_scr[:, pl.ds(r0, sub), :].reshape(PEER_TM, PEER_N_KEYS)
        wa.append((w * jax.nn.gelu(a[:, il * PEER_N_KEYS:(il + 1) * PEER_N_KEYS])).astype(_BF))
    acc_scr[...] += jnp.dot(jnp.concatenate(wa, axis=-1), v_ref[...], preferred_element_type=_F32)

    @pl.when(j == pl.num_programs(1) - 1)
    def _():
        h2_ref[...] = h1_ref[...] + acc_scr[...]


def _peer(eid, gate, h1, g_ffn, u_t_bf, v_bf):
    T = h1.shape[0]
    n_exp = v_bf.shape[0]
    te = PEER_I1_TILE * PEER_N_KEYS
    hk = PEER_HEADS * PEER_TOPK
    return pl.pallas_call(
        _peer_kernel,
        grid=(T // PEER_TM, n_exp // te),
        in_specs=[
            pl.BlockSpec((PEER_TM, hk), lambda i, j: (i, 0)),
            pl.BlockSpec((PEER_TM, hk), lambda i, j: (i, 0)),
            pl.BlockSpec((PEER_TM, D_MODEL), lambda i, j: (i, 0)),
            pl.BlockSpec((1, D_MODEL), lambda i, j: (0, 0)),
            pl.BlockSpec((D_MODEL, te), lambda i, j: (0, j)),
            pl.BlockSpec((te, D_MODEL), lambda i, j: (j, 0)),
        ],
        out_specs=pl.BlockSpec((PEER_TM, D_MODEL), lambda i, j: (i, 0)),
        out_shape=jax.ShapeDtypeStruct((T, D_MODEL), _F32),
        scratch_shapes=[
            pltpu.VMEM((PEER_TM // 8, PEER_N_KEYS * 8, PEER_N_KEYS), _F32),
            pltpu.VMEM((PEER_TM, D_MODEL), _BF),
            pltpu.VMEM((PEER_TM, D_MODEL), _F32),
        ],
        compiler_params=pltpu.CompilerParams(
            dimension_semantics=("parallel", "arbitrary"),
            vmem_limit_bytes=V7X_VMEM_LIMIT_BYTES),
        name="peer",
    )(eid, gate, h1, g_ffn, u_t_bf, v_bf)


def _ple_kernel(final, h_ref, p_ref, g_ref, wg_ref, wp_ref, gf_ref, out_ref):
    h = h_ref[...]
    gate = jax.nn.sigmoid(_mm(_rms(h, g_ref[...]), wg_ref[...]))
    h = h + _mm(p_ref[...], wp_ref[...]) * gate
    out_ref[...] = _rms(h, gf_ref[...]) if final else h


def _ple(h2, p2, g_ple, w_gate_bf, w_ple_bf, g_final, final):
    T = h2.shape[0]
    full = lambda shape: pl.BlockSpec(shape, lambda i: (0,) * len(shape))
    return pl.pallas_call(
        functools.partial(_ple_kernel, final),
        grid=(T // PLE_TM,),
        in_specs=[
            pl.BlockSpec((PLE_TM, D_MODEL), lambda i: (i, 0)),
            pl.BlockSpec((PLE_TM, PLE_DIM), lambda i: (i, 0)),
            full((1, D_MODEL)),
            full((D_MODEL, D_MODEL)),
            full((PLE_DIM, D_MODEL)),
            full((1, D_MODEL)),
        ],
        out_specs=pl.BlockSpec((PLE_TM, D_MODEL), lambda i: (i, 0)),
        out_shape=jax.ShapeDtypeStruct((T, D_MODEL), _F32),
        compiler_params=pltpu.CompilerParams(
            dimension_semantics=("parallel",), vmem_limit_bytes=V7X_VMEM_LIMIT_BYTES),
        name="ple",
    )(h2, p2, g_ple, w_gate_bf, w_ple_bf, g_final)


def kernel(x, p, g_mix, w_in, gmlp_ln_g, gmlp_ln_b, gmlp_w_s, gmlp_b_s, gmlp_beta, lambda_q1,
           lambda_k1, lambda_q2, lambda_k2, subln_g, rel_bias, w_out, g_ffn, peer_w_q, peer_keys,
           peer_u, peer_v, g_ple, w_ple, w_gate, g_final):
    B, S, D = x.shape
    depth = w_in.shape[0]
    T = B * S
    row = lambda a: a.reshape(1, -1)
    h = x.reshape(T, D)
    bias_tiles = _relbias(rel_bias)
    for i in range(depth):
        lam_init = 0.8 - 0.6 * math.exp(-0.3 * i)
        out_a, qkv = _mix_in(h, row(g_mix[i]), w_in[i].astype(_BF), row(gmlp_ln_g[i]),
                             row(gmlp_ln_b[i]), gmlp_w_s[i], jnp.transpose(gmlp_b_s[i]),
                             row(gmlp_beta[i]))
        out_b = _diffattn(qkv.reshape(B, S, QKV_COLS), bias_tiles, rel_bias, row(lambda_q1[i]),
                          row(lambda_k1[i]), row(lambda_q2[i]), row(lambda_k2[i]),
                          row(subln_g[i]), lam_init)
        h1, eid, gate = _route(h, out_a, out_b.reshape(T, DIFF_WIDTH), w_out[i].astype(_BF),
                               row(g_ffn[i]), peer_w_q[i].astype(_BF), peer_keys[i].astype(_BF))
        h2 = _peer(eid, gate, h1, row(g_ffn[i]), jnp.transpose(peer_u[i]).astype(_BF),
                   peer_v[i].astype(_BF))
        h = _ple(h2, p[i].reshape(T, PLE_DIM), row(g_ple[i]), w_gate[i].astype(_BF),
                 w_ple[i].astype(_BF), row(g_final), i == depth - 1)
    return h.reshape(B, S, D)
```

```python
import functools
import math

import jax
import jax.numpy as jnp
from jax import lax
from jax.experimental import pallas as pl
from jax.experimental.pallas import tpu as pltpu

D_MODEL = 1024
CHUNK = 64
GMLP_WIDTH = 512
GMLP_GROUPS = 4
GMLP_GROUP_CH = 128
GMLP_BLOCK = 128
DIFF_HEADS = 4
DIFF_HEAD_DIM = 64
DIFF_V_DIM = 128
DIFF_WIDTH = 512
A_COLS = 1024
QK_COLS = 1024
V_COLS = 512
IN_COLS = A_COLS + QK_COLS + V_COLS
REL_BUCKETS = 32
REL_MAX_EXACT = 8
PEER_N_KEYS = 128
PEER_HEADS = 8
PEER_TOPK = 16
PEER_HALF = 128
PEER_DKEY = 256
PEER_SLOTS = PEER_HEADS * PEER_TOPK
PLE_DIM = 256
EPS = 1e-6
NEG_INF = -1e30
SUBLANES = 8

V7X_VMEM_LIMIT_BYTES = 56 * 1024 * 1024

MIX_TM = 256
ATT_T = 512
RELBIAS_ROWS = 128
ROUTE_TM = 256
PEER_TM = 512
PEER_I1_TILE = 8
PEER_I1_BUILD = 64
PLE_TM = 512

_NT = (((1,), (1,)), ((), ()))
_BF = jnp.bfloat16
_F32 = jnp.float32


def _rms(x, g):
    return x * lax.rsqrt(jnp.mean(x * x, axis=-1, keepdims=True) + EPS) * g


def _mm(a, b):
    return jnp.dot(a.astype(_BF), b.astype(_BF), preferred_element_type=_F32)


def _mm_nt(a, b):
    return lax.dot_general(a.astype(_BF), b.astype(_BF), _NT, preferred_element_type=_F32)


def _mix_in_kernel(x_ref, g_ref, w_ref, lng_ref, lnb_ref, ws_ref, bst_ref, beta_ref,
                   outa_ref, qk_ref, vt_ref):
    n1 = _rms(x_ref[...], g_ref[...])
    z = jnp.dot(n1.astype(_BF), w_ref[...], preferred_element_type=_F32)
    qk_ref[...] = z[:, A_COLS:A_COLS + QK_COLS].astype(_BF)
    vt_ref[...] = z[:, A_COLS + QK_COLS:].T.astype(_BF)
    za = jax.nn.gelu(z[:, :A_COLS])
    pos_i = lax.broadcasted_iota(jnp.int32, (GMLP_BLOCK, GMLP_BLOCK), 0) // CHUNK
    pos_j = lax.broadcasted_iota(jnp.int32, (GMLP_BLOCK, GMLP_BLOCK), 1) // CHUNK
    causal = pos_j <= pos_i
    for g in range(GMLP_GROUPS):
        c0 = g * GMLP_GROUP_CH
        u = za[:, c0:c0 + GMLP_GROUP_CH]
        v = za[:, GMLP_WIDTH + c0:GMLP_WIDTH + c0 + GMLP_GROUP_CH]
        mu = jnp.mean(v, axis=-1, keepdims=True)
        vc = v - mu
        vn = vc * lax.rsqrt(jnp.mean(vc * vc, axis=-1, keepdims=True) + EPS)
        vn = vn * lng_ref[:, c0:c0 + GMLP_GROUP_CH] + lnb_ref[:, c0:c0 + GMLP_GROUP_CH]
        wm = jnp.where(causal, ws_ref[g], 0.0)
        for n in range(MIX_TM // GMLP_BLOCK):
            r0 = n * GMLP_BLOCK
            sv = _mm(wm, vn[r0:r0 + GMLP_BLOCK]) + bst_ref[:, g:g + 1]
            o = u[r0:r0 + GMLP_BLOCK] * sv
            outa_ref[r0:r0 + GMLP_BLOCK, c0:c0 + GMLP_GROUP_CH] = _rms(
                o, beta_ref[:, c0:c0 + GMLP_GROUP_CH])


def _mix_in(x2, seq, g_mix, w_in_bf, ln_g, ln_b, w_s, b_s_t, beta):
    T = x2.shape[0]
    n_seq = seq // MIX_TM
    full = lambda shape: pl.BlockSpec(shape, lambda i: (0,) * len(shape))
    return pl.pallas_call(
        _mix_in_kernel,
        grid=(T // MIX_TM,),
        in_specs=[
            pl.BlockSpec((MIX_TM, D_MODEL), lambda i: (i, 0)),
            full((1, D_MODEL)),
            full((D_MODEL, IN_COLS)),
            full((1, GMLP_WIDTH)),
            full((1, GMLP_WIDTH)),
            full((GMLP_GROUPS, GMLP_BLOCK, GMLP_BLOCK)),
            full((GMLP_BLOCK, GMLP_GROUPS)),
            full((1, GMLP_WIDTH)),
        ],
        out_specs=[
            pl.BlockSpec((MIX_TM, GMLP_WIDTH), lambda i: (i, 0)),
            pl.BlockSpec((MIX_TM, QK_COLS), lambda i: (i, 0)),
            pl.BlockSpec((None, V_COLS, MIX_TM), lambda i: (i // n_seq, 0, i % n_seq)),
        ],
        out_shape=[
            jax.ShapeDtypeStruct((T, GMLP_WIDTH), _F32),
            jax.ShapeDtypeStruct((T, QK_COLS), _BF),
            jax.ShapeDtypeStruct((T // seq, V_COLS, seq), _BF),
        ],
        compiler_params=pltpu.CompilerParams(
            dimension_semantics=("parallel",), vmem_limit_bytes=V7X_VMEM_LIMIT_BYTES),
        name="mix_in",
    )(x2, g_mix, w_in_bf, ln_g, ln_b, w_s, b_s_t, beta)


def _relbias_kernel(rb_ref, out_ref):
    h, d, r = pl.program_id(0), pl.program_id(1), pl.program_id(2)
    kj = lax.broadcasted_iota(jnp.int32, (RELBIAS_ROWS, ATT_T), 0) + r * RELBIAS_ROWS
    qi = lax.broadcasted_iota(jnp.int32, (RELBIAS_ROWS, ATT_T), 1)
    rel = kj - qi - d * ATT_T
    n = jnp.abs(rel)
    n2 = n * n
    large = jnp.full_like(n, REL_MAX_EXACT)
    for k in range(1, 8):
        large = large + (n2 >= (1 << (6 + k))).astype(jnp.int32)
    bucket = jnp.where(rel > 0, REL_BUCKETS // 2, 0) + jnp.where(n < REL_MAX_EXACT, n, large)
    bias = jnp.zeros(rel.shape, _F32)
    for b in range(REL_BUCKETS):
        bias = jnp.where(bucket == b, rb_ref[b, h], bias)
    bias = bias - rb_ref[REL_BUCKETS // 2 - 1, h]
    visible = ((kj - d * ATT_T) // CHUNK) <= (qi // CHUNK)
    out_ref[...] = jnp.where(visible, bias, NEG_INF)


def _relbias(rel_bias):
    return pl.pallas_call(
        _relbias_kernel,
        grid=(DIFF_HEADS, 2, ATT_T // RELBIAS_ROWS),
        in_specs=[pl.BlockSpec(memory_space=pltpu.SMEM)],
        out_specs=pl.BlockSpec((None, None, RELBIAS_ROWS, ATT_T), lambda h, d, r: (h, d, r, 0)),
        out_shape=jax.ShapeDtypeStruct((DIFF_HEADS, 2, ATT_T, ATT_T), _F32),
        name="relbias",
    )(rel_bias)


def _diffattn_kernel(lam_init, q_ref, k_ref, vt_ref, bias_ref, lq1_ref, lk1_ref, lq2_ref, lk2_ref,
                     sg_ref, out_ref, m_scr, l_scr, acc_scr):
    qi = pl.program_id(2)
    q = q_ref[...] * (DIFF_HEAD_DIM ** -0.5)
    lane = lax.broadcasted_iota(jnp.int32, q.shape, 1)
    zero = jnp.zeros_like(q)
    q2x = jnp.concatenate([jnp.where(lane < DIFF_HEAD_DIM, q, zero),
                           jnp.where(lane >= DIFF_HEAD_DIM, q, zero)], axis=0)

    def step(j, bias, first):
        r0 = pl.multiple_of(j * ATT_T, ATT_T)
        kt = k_ref[pl.ds(r0, ATT_T), :]
        vt = vt_ref[:, pl.ds(r0, ATT_T)]
        s = lax.dot_general(kt, q2x, _NT, preferred_element_type=_F32)
        if bias is not None:
            s = s + jnp.concatenate([bias, bias], axis=1)
        s_max = jnp.max(s, axis=0, keepdims=True)
        if first:
            m_new = s_max
            p = jnp.exp(s - m_new)
            l_scr[...] = jnp.sum(p, axis=0, keepdims=True)
            acc_scr[...] = jnp.dot(vt, p.astype(_BF), preferred_element_type=_F32)
        else:
            m_old = m_scr[...]
            m_new = jnp.maximum(m_old, s_max)
            a = jnp.exp(m_old - m_new)
            p = jnp.exp(s - m_new)
            l_scr[...] = a * l_scr[...] + jnp.sum(p, axis=0, keepdims=True)
            acc_scr[...] = a * acc_scr[...] + jnp.dot(vt, p.astype(_BF),
                                                      preferred_element_type=_F32)
        m_scr[...] = m_new

    step(qi, bias_ref[0], True)

    @pl.when(qi > 0)
    def _():
        step(qi - 1, bias_ref[1], False)

    def far(j, _):
        step(j, None, False)
        return 0

    lax.fori_loop(0, qi - 1, far, 0)

    lam = (jnp.exp(jnp.sum(lq1_ref[...] * lk1_ref[...], axis=-1, keepdims=True))
           - jnp.exp(jnp.sum(lq2_ref[...] * lk2_ref[...], axis=-1, keepdims=True)) + lam_init)
    o = acc_scr[...] / l_scr[...]
    o = o[:, :ATT_T] - lam * o[:, ATT_T:]
    o = o * lax.rsqrt(jnp.mean(o * o, axis=0, keepdims=True) + EPS)
    out_ref[...] = o.T * sg_ref[...] * (1.0 - lam_init)


def _diffattn(qk3, vt3, bias_tiles, lq1, lk1, lq2, lk2, subln_g, lam_init):
    B, S, _ = qk3.shape
    nq = S // ATT_T
    vec = lambda n: pl.BlockSpec((1, n), lambda b, h, i: (0, 0))
    return pl.pallas_call(
        functools.partial(_diffattn_kernel, lam_init),
        grid=(B, DIFF_HEADS, nq),
        in_specs=[
            pl.BlockSpec((None, ATT_T, 128), lambda b, h, i: (b, i, h)),
            pl.BlockSpec((None, S, 128), lambda b, h, i: (b, 0, DIFF_HEADS + h)),
            pl.BlockSpec((None, DIFF_V_DIM, S), lambda b, h, i: (b, h, 0)),
            pl.BlockSpec((None, 2, ATT_T, ATT_T), lambda b, h, i: (h, 0, 0, 0)),
            vec(DIFF_HEAD_DIM), vec(DIFF_HEAD_DIM), vec(DIFF_HEAD_DIM), vec(DIFF_HEAD_DIM),
            vec(DIFF_V_DIM),
        ],
        out_specs=pl.BlockSpec((None, ATT_T, DIFF_V_DIM), lambda b, h, i: (b, i, h)),
        out_shape=jax.ShapeDtypeStruct((B, S, DIFF_WIDTH), _F32),
        scratch_shapes=[
            pltpu.VMEM((1, 2 * ATT_T), _F32),
            pltpu.VMEM((1, 2 * ATT_T), _F32),
            pltpu.VMEM((DIFF_V_DIM, 2 * ATT_T), _F32),
        ],
        compiler_params=pltpu.CompilerParams(
            dimension_semantics=("parallel", "parallel", "arbitrary"),
            vmem_limit_bytes=V7X_VMEM_LIMIT_BYTES),
        name="diffattn",
    )(qk3, qk3, vt3, bias_tiles, lq1, lk1, lq2, lk2, subln_g)


def _top16(s, fill):
    n_rows = s.shape[0]
    iota = lax.broadcasted_iota(jnp.int32, s.shape, 0)
    vals, poss = [], []
    for _ in range(PEER_TOPK):
        m = jnp.max(s, axis=0, keepdims=True)
        pos = jnp.min(jnp.where(s == m, iota, n_rows), axis=0, keepdims=True)
        vals.append(m)
        poss.append(pos)
        s = jnp.where(iota == pos, fill, s)
    return vals, poss


def _candidates(v1, i1, v2, i2):
    neg = -jnp.inf
    v1c, i1c = jnp.concatenate(v1, axis=0), jnp.concatenate(i1, axis=0)
    v2c, i2c = jnp.concatenate(v2, axis=0), jnp.concatenate(i2, axis=0)
    row = lax.broadcasted_iota(jnp.int32, (SUBLANES, v1c.shape[1]), 0)
    cand = [v1[0] + v2c]
    cidx = [i1[0] * PEER_N_KEYS + i2c]
    for a in range(1, SUBLANES):
        nb = PEER_TOPK // (a + 1)
        cand.append(jnp.where(row < nb, v1[a] + v2c[:SUBLANES], neg))
        cidx.append(i1[a] * PEER_N_KEYS + i2c[:SUBLANES])
    cand.append(v1c[SUBLANES:] + v2[0])
    cidx.append(i1c[SUBLANES:] * PEER_N_KEYS + i2[0])
    return jnp.concatenate(cand, axis=0), jnp.concatenate(cidx, axis=0)


def _route_kernel(x_ref, oa_ref, ob_ref, wo_ref, g_ref, wq_ref, keys_ref,
                  h1_ref, eid_ref, gate_ref, qp_scr, eid_scr, gate_scr):
    mix = jnp.concatenate([oa_ref[...], ob_ref[...]], axis=-1)
    h1 = x_ref[...] + _mm(mix, wo_ref[...])
    h1_ref[...] = h1
    xn = _rms(h1, g_ref[...])
    qp_scr[...] = _mm(xn, wq_ref[...])
    neg = -jnp.inf

    def head(h, _):
        c0 = pl.multiple_of(h * PEER_DKEY, PEER_DKEY)
        q1 = qp_scr[:, pl.ds(c0, PEER_HALF)]
        q2 = qp_scr[:, pl.ds(c0 + PEER_HALF, PEER_HALF)]
        v1, i1 = _top16(_mm_nt(keys_ref[h, 0], q1), neg)
        v2, i2 = _top16(_mm_nt(keys_ref[h, 1], q2), neg)
        cand, cidx = _candidates(v1, i1, v2, i2)
        best, pos = _top16(cand, neg)
        iota = lax.broadcasted_iota(jnp.int32, cand.shape, 0)
        eid = jnp.concatenate(
            [jnp.max(jnp.where(iota == p, cidx, -1), axis=0, keepdims=True) for p in pos], axis=0)
        best = jnp.concatenate(best, axis=0)
        e = jnp.exp(best - best[0:1])
        gate = e / jnp.sum(e, axis=0, keepdims=True)
        r0 = pl.multiple_of(h * PEER_TOPK, PEER_TOPK)
        eid_scr[pl.ds(r0, PEER_TOPK), :] = eid
        gate_scr[pl.ds(r0, PEER_TOPK), :] = gate
        return 0

    lax.fori_loop(0, PEER_HEADS, head, 0)
    eid_ref[...] = eid_scr[...].T
    gate_ref[...] = gate_scr[...].T


def _route(x2, out_a, out_b, w_out_bf, g_ffn, w_q_bf, keys_bf):
    T = x2.shape[0]
    tm = ROUTE_TM
    full = lambda shape: pl.BlockSpec(shape, lambda i: (0,) * len(shape))
    return pl.pallas_call(
        _route_kernel,
        grid=(T // tm,),
        in_specs=[
            pl.BlockSpec((tm, D_MODEL), lambda i: (i, 0)),
            pl.BlockSpec((tm, GMLP_WIDTH), lambda i: (i, 0)),
            pl.BlockSpec((tm, DIFF_WIDTH), lambda i: (i, 0)),
            full((D_MODEL, D_MODEL)),
            full((1, D_MODEL)),
            full((D_MODEL, PEER_HEADS * PEER_DKEY)),
            full((PEER_HEADS, 2, PEER_N_KEYS, PEER_HALF)),
        ],
        out_specs=[
            pl.BlockSpec((tm, D_MODEL), lambda i: (i, 0)),
            pl.BlockSpec((tm, PEER_SLOTS), lambda i: (i, 0)),
            pl.BlockSpec((tm, PEER_SLOTS), lambda i: (i, 0)),
        ],
        out_shape=[
            jax.ShapeDtypeStruct((T, D_MODEL), _F32),
            jax.ShapeDtypeStruct((T, PEER_SLOTS), jnp.int32),
            jax.ShapeDtypeStruct((T, PEER_SLOTS), _F32),
        ],
        scratch_shapes=[
            pltpu.VMEM((tm, PEER_HEADS * PEER_DKEY), _F32),
            pltpu.VMEM((PEER_SLOTS, tm), jnp.int32),
            pltpu.VMEM((PEER_SLOTS, tm), _F32),
        ],
        compiler_params=pltpu.CompilerParams(
            dimension_semantics=("parallel",), vmem_limit_bytes=V7X_VMEM_LIMIT_BYTES),
        name="route",
    )(x2, out_a, out_b, w_out_bf, g_ffn, w_q_bf, keys_bf)


def _peer_kernel(eid_ref, gate_ref, h1_ref, g_ref, ut_ref, v_ref, h2_ref, w_scr, xn_scr, acc_scr):
    j = pl.program_id(1)
    tiles_per_build = PEER_I1_BUILD // PEER_I1_TILE

    @pl.when(j == 0)
    def _():
        xn_scr[...] = _rms(h1_ref[...], g_ref[...]).astype(_BF)
        acc_scr[...] = jnp.zeros_like(acc_scr)

    @pl.when(j % tiles_per_build == 0)
    def _():
        i1_base = (j // tiles_per_build) * PEER_I1_BUILD
        iota1 = lax.broadcasted_iota(jnp.int32, (PEER_I1_BUILD, PEER_SLOTS), 0) + i1_base
        iota2 = lax.broadcasted_iota(jnp.int32, (PEER_N_KEYS, PEER_SLOTS), 0)

        def group(tg, _):
            for tl in range(SUBLANES):
                t = tg * SUBLANES + tl
                e_row = eid_ref[pl.ds(t, 1), :]
                pt = jnp.where(iota1 == (e_row >> 7), gate_ref[pl.ds(t, 1), :], 0.0)
                qt = jnp.where(iota2 == (e_row & (PEER_N_KEYS - 1)), 1.0, 0.0)
                w_scr[tg, pl.ds(tl, PEER_I1_BUILD, stride=SUBLANES), :] = _mm_nt(pt, qt)
            return 0

        lax.fori_loop(0, PEER_TM // SUBLANES, group, 0)

    a = jnp.dot(xn_scr[...], ut_ref[...], preferred_element_type=_F32)
    wa = []
    for il in range(PEER_I1_TILE):
        r0 = pl.multiple_of(((j % tiles_per_build) * PEER_I1_TILE + il) * SUBLANES, SUBLANES)
        w = w_scr[:, pl.ds(r0, SUBLANES), :].reshape(PEER_TM, PEER_N_KEYS)
        wa.append((w * jax.nn.gelu(a[:, il * PEER_N_KEYS:(il + 1) * PEER_N_KEYS])).astype(_BF))
    acc_scr[...] += jnp.dot(jnp.concatenate(wa, axis=-1), v_ref[...], preferred_element_type=_F32)

    @pl.when(j == pl.num_programs(1) - 1)
    def _():
        h2_ref[...] = h1_ref[...] + acc_scr[...]


def _peer(eid, gate, h1, g_ffn, u_t_bf, v_bf):
    T = h1.shape[0]
    n_exp = v_bf.shape[0]
    te = PEER_I1_TILE * PEER_N_KEYS
    return pl.pallas_call(
        _peer_kernel,
        grid=(T // PEER_TM, n_exp // te),
        in_specs=[
            pl.BlockSpec((PEER_TM, PEER_SLOTS), lambda i, j: (i, 0)),
            pl.BlockSpec((PEER_TM, PEER_SLOTS), lambda i, j: (i, 0)),
            pl.BlockSpec((PEER_TM, D_MODEL), lambda i, j: (i, 0)),
            pl.BlockSpec((1, D_MODEL), lambda i, j: (0, 0)),
            pl.BlockSpec((D_MODEL, te), lambda i, j: (0, j)),
            pl.BlockSpec((te, D_MODEL), lambda i, j: (j, 0)),
        ],
        out_specs=pl.BlockSpec((PEER_TM, D_MODEL), lambda i, j: (i, 0)),
        out_shape=jax.ShapeDtypeStruct((T, D_MODEL), _F32),
        scratch_shapes=[
            pltpu.VMEM((PEER_TM // SUBLANES, PEER_I1_BUILD * SUBLANES, PEER_N_KEYS), _F32),
            pltpu.VMEM((PEER_TM, D_MODEL), _BF),
            pltpu.VMEM((PEER_TM, D_MODEL), _F32),
        ],
        compiler_params=pltpu.CompilerParams(
            dimension_semantics=("parallel", "arbitrary"),
            vmem_limit_bytes=V7X_VMEM_LIMIT_BYTES),
        name="peer",
    )(eid, gate, h1, g_ffn, u_t_bf, v_bf)


def _ple_kernel(final, h_ref, p_ref, g_ref, wg_ref, wp_ref, gf_ref, out_ref):
    h = h_ref[...]
    gate = jax.nn.sigmoid(_mm(_rms(h, g_ref[...]), wg_ref[...]))
    h = h + _mm(p_ref[...], wp_ref[...]) * gate
    out_ref[...] = _rms(h, gf_ref[...]) if final else h


def _ple(h2, p2, g_ple, w_gate_bf, w_ple_bf, g_final, final):
    T = h2.shape[0]
    full = lambda shape: pl.BlockSpec(shape, lambda i: (0,) * len(shape))
    return pl.pallas_call(
        functools.partial(_ple_kernel, final),
        grid=(T // PLE_TM,),
        in_specs=[
            pl.BlockSpec((PLE_TM, D_MODEL), lambda i: (i, 0)),
            pl.BlockSpec((PLE_TM, PLE_DIM), lambda i: (i, 0)),
            full((1, D_MODEL)),
            full((D_MODEL, D_MODEL)),
            full((PLE_DIM, D_MODEL)),
            full((1, D_MODEL)),
        ],
        out_specs=pl.BlockSpec((PLE_TM, D_MODEL), lambda i: (i, 0)),
        out_shape=jax.ShapeDtypeStruct((T, D_MODEL), _F32),
        compiler_params=pltpu.CompilerParams(
            dimension_semantics=("parallel",), vmem_limit_bytes=V7X_VMEM_LIMIT_BYTES),
        name="ple",
    )(h2, p2, g_ple, w_gate_bf, w_ple_bf, g_final)


def kernel(x, p, g_mix, w_in, gmlp_ln_g, gmlp_ln_b, gmlp_w_s, gmlp_b_s, gmlp_beta, lambda_q1,
           lambda_k1, lambda_q2, lambda_k2, subln_g, rel_bias, w_out, g_ffn, peer_w_q, peer_keys,
           peer_u, peer_v, g_ple, w_ple, w_gate, g_final):
    B, S, D = x.shape
    depth = w_in.shape[0]
    T = B * S
    row = lambda a: a.reshape(1, -1)
    h = x.reshape(T, D)
    bias_tiles = _relbias(rel_bias)
    for i in range(depth):
        lam_init = 0.8 - 0.6 * math.exp(-0.3 * i)
        out_a, qk, vt = _mix_in(h, S, row(g_mix[i]), w_in[i].astype(_BF), row(gmlp_ln_g[i]),
                                row(gmlp_ln_b[i]), gmlp_w_s[i], jnp.transpose(gmlp_b_s[i]),
                                row(gmlp_beta[i]))
        out_b = _diffattn(qk.reshape(B, S, QK_COLS), vt, bias_tiles, row(lambda_q1[i]),
                          row(lambda_k1[i]), row(lambda_q2[i]), row(lambda_k2[i]),
                          row(subln_g[i]), lam_init)
        h1, eid, gate = _route(h, out_a, out_b.reshape(T, DIFF_WIDTH), w_out[i].astype(_BF),
                               row(g_ffn[i]), peer_w_q[i].astype(_BF), peer_keys[i].astype(_BF))
        h2 = _peer(eid, gate, h1, row(g_ffn[i]), jnp.transpose(peer_u[i]).astype(_BF),
                   peer_v[i].astype(_BF))
        h = _ple(h2, p[i].reshape(T, PLE_DIM), row(g_ple[i]), w_gate[i].astype(_BF),
                 w_ple[i].astype(_BF), row(g_final), i == depth - 1)
    return h.reshape(B, S, D)
```

```python
import functools
import math

import jax
import jax.numpy as jnp
from jax import lax
from jax.experimental import pallas as pl
from jax.experimental.pallas import tpu as pltpu

D_MODEL = 1024
CHUNK = 64
GMLP_WIDTH = 512
GMLP_GROUPS = 4
GMLP_GROUP_CH = 128
GMLP_BLOCK = 128
DIFF_HEADS = 4
DIFF_HEAD_DIM = 64
DIFF_V_DIM = 128
DIFF_WIDTH = 512
A_COLS = 1024
QK_COLS = 1024
V_COLS = 512
IN_COLS = A_COLS + QK_COLS + V_COLS
REL_BUCKETS = 32
REL_MAX_EXACT = 8
PEER_N_KEYS = 128
PEER_HEADS = 8
PEER_TOPK = 16
PEER_HALF = 128
PEER_DKEY = 256
PEER_SLOTS = PEER_HEADS * PEER_TOPK
PLE_DIM = 256
EPS = 1e-6
NEG_INF = -1e30
SUBLANES = 8

V7X_VMEM_LIMIT_BYTES = 56 * 1024 * 1024

MIX_TM = 256
ATT_T = 512
RELBIAS_ROWS = 128
ROUTE_TM = 256
PEER_TM = 512
PEER_I1_TILE = 8
PEER_BUILD_TOKENS = 2
PLE_TM = 512

_NT = (((1,), (1,)), ((), ()))
_BF = jnp.bfloat16
_F32 = jnp.float32


def _rms(x, g):
    return x * lax.rsqrt(jnp.mean(x * x, axis=-1, keepdims=True) + EPS) * g


def _mm(a, b):
    return jnp.dot(a.astype(_BF), b.astype(_BF), preferred_element_type=_F32)


def _mm_nt(a, b):
    return lax.dot_general(a.astype(_BF), b.astype(_BF), _NT, preferred_element_type=_F32)


def _mix_in_kernel(x_ref, g_ref, w_ref, lng_ref, lnb_ref, ws_ref, bst_ref, beta_ref,
                   outa_ref, qk_ref, vt_ref):
    n1 = _rms(x_ref[...], g_ref[...])
    z = jnp.dot(n1.astype(_BF), w_ref[...], preferred_element_type=_F32)
    qk_ref[...] = z[:, A_COLS:A_COLS + QK_COLS].astype(_BF)
    vt_ref[...] = z[:, A_COLS + QK_COLS:].T.astype(_BF)
    za = jax.nn.gelu(z[:, :A_COLS])
    pos_i = lax.broadcasted_iota(jnp.int32, (GMLP_BLOCK, GMLP_BLOCK), 0) // CHUNK
    pos_j = lax.broadcasted_iota(jnp.int32, (GMLP_BLOCK, GMLP_BLOCK), 1) // CHUNK
    causal = pos_j <= pos_i
    for g in range(GMLP_GROUPS):
        c0 = g * GMLP_GROUP_CH
        u = za[:, c0:c0 + GMLP_GROUP_CH]
        v = za[:, GMLP_WIDTH + c0:GMLP_WIDTH + c0 + GMLP_GROUP_CH]
        mu = jnp.mean(v, axis=-1, keepdims=True)
        vc = v - mu
        vn = vc * lax.rsqrt(jnp.mean(vc * vc, axis=-1, keepdims=True) + EPS)
        vn = vn * lng_ref[:, c0:c0 + GMLP_GROUP_CH] + lnb_ref[:, c0:c0 + GMLP_GROUP_CH]
        wm = jnp.where(causal, ws_ref[g], 0.0)
        for n in range(MIX_TM // GMLP_BLOCK):
            r0 = n * GMLP_BLOCK
            sv = _mm(wm, vn[r0:r0 + GMLP_BLOCK]) + bst_ref[:, g:g + 1]
            o = u[r0:r0 + GMLP_BLOCK] * sv
            outa_ref[r0:r0 + GMLP_BLOCK, c0:c0 + GMLP_GROUP_CH] = _rms(
                o, beta_ref[:, c0:c0 + GMLP_GROUP_CH])


def _mix_in(x2, seq, g_mix, w_in_bf, ln_g, ln_b, w_s, b_s_t, beta):
    T = x2.shape[0]
    n_seq = seq // MIX_TM
    full = lambda shape: pl.BlockSpec(shape, lambda i: (0,) * len(shape))
    return pl.pallas_call(
        _mix_in_kernel,
        grid=(T // MIX_TM,),
        in_specs=[
            pl.BlockSpec((MIX_TM, D_MODEL), lambda i: (i, 0)),
            full((1, D_MODEL)),
            full((D_MODEL, IN_COLS)),
            full((1, GMLP_WIDTH)),
            full((1, GMLP_WIDTH)),
            full((GMLP_GROUPS, GMLP_BLOCK, GMLP_BLOCK)),
            full((GMLP_BLOCK, GMLP_GROUPS)),
            full((1, GMLP_WIDTH)),
        ],
        out_specs=[
            pl.BlockSpec((MIX_TM, GMLP_WIDTH), lambda i: (i, 0)),
            pl.BlockSpec((MIX_TM, QK_COLS), lambda i: (i, 0)),
            pl.BlockSpec((None, V_COLS, MIX_TM), lambda i: (i // n_seq, 0, i % n_seq)),
        ],
        out_shape=[
            jax.ShapeDtypeStruct((T, GMLP_WIDTH), _F32),
            jax.ShapeDtypeStruct((T, QK_COLS), _BF),
            jax.ShapeDtypeStruct((T // seq, V_COLS, seq), _BF),
        ],
        compiler_params=pltpu.CompilerParams(
            dimension_semantics=("parallel",), vmem_limit_bytes=V7X_VMEM_LIMIT_BYTES),
        name="mix_in",
    )(x2, g_mix, w_in_bf, ln_g, ln_b, w_s, b_s_t, beta)


def _relbias_kernel(rb_ref, out_ref):
    h, d, r = pl.program_id(0), pl.program_id(1), pl.program_id(2)
    kj = lax.broadcasted_iota(jnp.int32, (RELBIAS_ROWS, ATT_T), 0) + r * RELBIAS_ROWS
    qi = lax.broadcasted_iota(jnp.int32, (RELBIAS_ROWS, ATT_T), 1)
    rel = kj - qi - d * ATT_T
    n = jnp.abs(rel)
    n2 = n * n
    large = jnp.full_like(n, REL_MAX_EXACT)
    for k in range(1, 8):
        large = large + (n2 >= (1 << (6 + k))).astype(jnp.int32)
    bucket = jnp.where(rel > 0, REL_BUCKETS // 2, 0) + jnp.where(n < REL_MAX_EXACT, n, large)
    bias = jnp.zeros(rel.shape, _F32)
    for b in range(REL_BUCKETS):
        bias = jnp.where(bucket == b, rb_ref[b, h], bias)
    bias = bias - rb_ref[REL_BUCKETS // 2 - 1, h]
    visible = ((kj - d * ATT_T) // CHUNK) <= (qi // CHUNK)
    out_ref[...] = jnp.where(visible, bias, NEG_INF)


def _relbias(rel_bias):
    return pl.pallas_call(
        _relbias_kernel,
        grid=(DIFF_HEADS, 2, ATT_T // RELBIAS_ROWS),
        in_specs=[pl.BlockSpec(memory_space=pltpu.SMEM)],
        out_specs=pl.BlockSpec((None, None, RELBIAS_ROWS, ATT_T), lambda h, d, r: (h, d, r, 0)),
        out_shape=jax.ShapeDtypeStruct((DIFF_HEADS, 2, ATT_T, ATT_T), _F32),
        name="relbias",
    )(rel_bias)


def _diffattn_kernel(lam_init, q_ref, k_ref, vt_ref, bias_ref, lq1_ref, lk1_ref, lq2_ref, lk2_ref,
                     sg_ref, out_ref, m_scr, l_scr, acc_scr):
    qi = pl.program_id(2)
    q = q_ref[...] * (DIFF_HEAD_DIM ** -0.5)
    lane = lax.broadcasted_iota(jnp.int32, q.shape, 1)
    zero = jnp.zeros_like(q)
    q2x = jnp.concatenate([jnp.where(lane < DIFF_HEAD_DIM, q, zero),
                           jnp.where(lane >= DIFF_HEAD_DIM, q, zero)], axis=0)

    def step(j, bias, first):
        r0 = pl.multiple_of(j * ATT_T, ATT_T)
        kt = k_ref[pl.ds(r0, ATT_T), :]
        vt = vt_ref[:, pl.ds(r0, ATT_T)]
        s = lax.dot_general(kt, q2x, _NT, preferred_element_type=_F32)
        if bias is not None:
            s = s + jnp.concatenate([bias, bias], axis=1)
        s_max = jnp.max(s, axis=0, keepdims=True)
        if first:
            m_new = s_max
            p = jnp.exp(s - m_new)
            l_scr[...] = jnp.sum(p, axis=0, keepdims=True)
            acc_scr[...] = jnp.dot(vt, p.astype(_BF), preferred_element_type=_F32)
        else:
            m_old = m_scr[...]
            m_new = jnp.maximum(m_old, s_max)
            a = jnp.exp(m_old - m_new)
            p = jnp.exp(s - m_new)
            l_scr[...] = a * l_scr[...] + jnp.sum(p, axis=0, keepdims=True)
            acc_scr[...] = a * acc_scr[...] + jnp.dot(vt, p.astype(_BF),
                                                      preferred_element_type=_F32)
        m_scr[...] = m_new

    step(qi, bias_ref[0], True)

    @pl.when(qi > 0)
    def _():
        step(qi - 1, bias_ref[1], False)

    def far(j, _):
        step(j, None, False)
        return 0

    lax.fori_loop(0, qi - 1, far, 0)

    lam = (jnp.exp(jnp.sum(lq1_ref[...] * lk1_ref[...], axis=-1, keepdims=True))
           - jnp.exp(jnp.sum(lq2_ref[...] * lk2_ref[...], axis=-1, keepdims=True)) + lam_init)
    o = acc_scr[...] / l_scr[...]
    o = o[:, :ATT_T] - lam * o[:, ATT_T:]
    o = o * lax.rsqrt(jnp.mean(o * o, axis=0, keepdims=True) + EPS)
    out_ref[...] = o.T * sg_ref[...] * (1.0 - lam_init)


def _diffattn(qk3, vt3, bias_tiles, lq1, lk1, lq2, lk2, subln_g, lam_init):
    B, S, _ = qk3.shape
    nq = S // ATT_T
    vec = lambda n: pl.BlockSpec((1, n), lambda b, h, i: (0, 0))
    return pl.pallas_call(
        functools.partial(_diffattn_kernel, lam_init),
        grid=(B, DIFF_HEADS, nq),
        in_specs=[
            pl.BlockSpec((None, ATT_T, 128), lambda b, h, i: (b, i, h)),
            pl.BlockSpec((None, S, 128), lambda b, h, i: (b, 0, DIFF_HEADS + h)),
            pl.BlockSpec((None, DIFF_V_DIM, S), lambda b, h, i: (b, h, 0)),
            pl.BlockSpec((None, 2, ATT_T, ATT_T), lambda b, h, i: (h, 0, 0, 0)),
            vec(DIFF_HEAD_DIM), vec(DIFF_HEAD_DIM), vec(DIFF_HEAD_DIM), vec(DIFF_HEAD_DIM),
            vec(DIFF_V_DIM),
        ],
        out_specs=pl.BlockSpec((None, ATT_T, DIFF_V_DIM), lambda b, h, i: (b, i, h)),
        out_shape=jax.ShapeDtypeStruct((B, S, DIFF_WIDTH), _F32),
        scratch_shapes=[
            pltpu.VMEM((1, 2 * ATT_T), _F32),
            pltpu.VMEM((1, 2 * ATT_T), _F32),
            pltpu.VMEM((DIFF_V_DIM, 2 * ATT_T), _F32),
        ],
        compiler_params=pltpu.CompilerParams(
            dimension_semantics=("parallel", "parallel", "arbitrary"),
            vmem_limit_bytes=V7X_VMEM_LIMIT_BYTES),
        name="diffattn",
    )(qk3, qk3, vt3, bias_tiles, lq1, lk1, lq2, lk2, subln_g)


def _top16(s, fill):
    n_rows = s.shape[0]
    iota = lax.broadcasted_iota(jnp.int32, s.shape, 0)
    vals, poss = [], []
    for _ in range(PEER_TOPK):
        m = jnp.max(s, axis=0, keepdims=True)
        pos = jnp.min(jnp.where(s == m, iota, n_rows), axis=0, keepdims=True)
        vals.append(m)
        poss.append(pos)
        s = jnp.where(iota == pos, fill, s)
    return vals, poss


def _candidates(v1, i1, v2, i2):
    neg = -jnp.inf
    v1c, i1c = jnp.concatenate(v1, axis=0), jnp.concatenate(i1, axis=0)
    v2c, i2c = jnp.concatenate(v2, axis=0), jnp.concatenate(i2, axis=0)
    row = lax.broadcasted_iota(jnp.int32, (SUBLANES, v1c.shape[1]), 0)
    cand = [v1[0] + v2c]
    cidx = [i1[0] * PEER_N_KEYS + i2c]
    for a in range(1, SUBLANES):
        nb = PEER_TOPK // (a + 1)
        cand.append(jnp.where(row < nb, v1[a] + v2c[:SUBLANES], neg))
        cidx.append(i1[a] * PEER_N_KEYS + i2c[:SUBLANES])
    cand.append(v1c[SUBLANES:] + v2[0])
    cidx.append(i1c[SUBLANES:] * PEER_N_KEYS + i2[0])
    return jnp.concatenate(cand, axis=0), jnp.concatenate(cidx, axis=0)


def _route_kernel(x_ref, oa_ref, ob_ref, wo_ref, g_ref, wq_ref, keys_ref,
                  h1_ref, eid_ref, gate_ref, qp_scr, eid_scr, gate_scr):
    mix = jnp.concatenate([oa_ref[...], ob_ref[...]], axis=-1)
    h1 = x_ref[...] + _mm(mix, wo_ref[...])
    h1_ref[...] = h1
    xn = _rms(h1, g_ref[...])
    qp_scr[...] = _mm(xn, wq_ref[...])
    neg = -jnp.inf

    def head(h, _):
        c0 = pl.multiple_of(h * PEER_DKEY, PEER_DKEY)
        q1 = qp_scr[:, pl.ds(c0, PEER_HALF)]
        q2 = qp_scr[:, pl.ds(c0 + PEER_HALF, PEER_HALF)]
        v1, i1 = _top16(_mm_nt(keys_ref[h, 0], q1), neg)
        v2, i2 = _top16(_mm_nt(keys_ref[h, 1], q2), neg)
        cand, cidx = _candidates(v1, i1, v2, i2)
        best, pos = _top16(cand, neg)
        iota = lax.broadcasted_iota(jnp.int32, cand.shape, 0)
        eid = jnp.concatenate(
            [jnp.max(jnp.where(iota == p, cidx, -1), axis=0, keepdims=True) for p in pos], axis=0)
        best = jnp.concatenate(best, axis=0)
        e = jnp.exp(best - best[0:1])
        gate = e / jnp.sum(e, axis=0, keepdims=True)
        r0 = pl.multiple_of(h * PEER_TOPK, PEER_TOPK)
        eid_scr[pl.ds(r0, PEER_TOPK), :] = eid
        gate_scr[pl.ds(r0, PEER_TOPK), :] = gate
        return 0

    lax.fori_loop(0, PEER_HEADS, head, 0)
    eid_ref[...] = eid_scr[...].T
    gate_ref[...] = gate_scr[...].T


def _route(x2, out_a, out_b, w_out_bf, g_ffn, w_q_bf, keys_bf):
    T = x2.shape[0]
    tm = ROUTE_TM
    full = lambda shape: pl.BlockSpec(shape, lambda i: (0,) * len(shape))
    return pl.pallas_call(
        _route_kernel,
        grid=(T // tm,),
        in_specs=[
            pl.BlockSpec((tm, D_MODEL), lambda i: (i, 0)),
            pl.BlockSpec((tm, GMLP_WIDTH), lambda i: (i, 0)),
            pl.BlockSpec((tm, DIFF_WIDTH), lambda i: (i, 0)),
            full((D_MODEL, D_MODEL)),
            full((1, D_MODEL)),
            full((D_MODEL, PEER_HEADS * PEER_DKEY)),
            full((PEER_HEADS, 2, PEER_N_KEYS, PEER_HALF)),
        ],
        out_specs=[
            pl.BlockSpec((tm, D_MODEL), lambda i: (i, 0)),
            pl.BlockSpec((tm, PEER_SLOTS), lambda i: (i, 0)),
            pl.BlockSpec((tm, PEER_SLOTS), lambda i: (i, 0)),
        ],
        out_shape=[
            jax.ShapeDtypeStruct((T, D_MODEL), _F32),
            jax.ShapeDtypeStruct((T, PEER_SLOTS), jnp.int32),
            jax.ShapeDtypeStruct((T, PEER_SLOTS), _F32),
        ],
        scratch_shapes=[
            pltpu.VMEM((tm, PEER_HEADS * PEER_DKEY), _F32),
            pltpu.VMEM((PEER_SLOTS, tm), jnp.int32),
            pltpu.VMEM((PEER_SLOTS, tm), _F32),
        ],
        compiler_params=pltpu.CompilerParams(
            dimension_semantics=("parallel",), vmem_limit_bytes=V7X_VMEM_LIMIT_BYTES),
        name="route",
    )(x2, out_a, out_b, w_out_bf, g_ffn, w_q_bf, keys_bf)


def _peer_kernel(eid_ref, gate_ref, h1_ref, g_ref, u_ref, v_ref, h2_ref, w_scr, xn_scr):
    j = pl.program_id(1)

    @pl.when(j == 0)
    def _():
        h1 = h1_ref[...]
        xn_scr[...] = _rms(h1, g_ref[...]).astype(_BF)
        h2_ref[...] = h1
        iota = lax.broadcasted_iota(jnp.int32, (PEER_N_KEYS, PEER_SLOTS), 0)
        zeros = jnp.zeros((PEER_N_KEYS, PEER_SLOTS), _BF)

        def onehots(t):
            e_row = eid_ref[pl.ds(t, 1), :]
            pt = jnp.where(iota == (e_row >> 7), gate_ref[pl.ds(t, 1), :], 0.0).astype(_BF)
            qt = jnp.where(iota == (e_row & (PEER_N_KEYS - 1)), 1.0, 0.0).astype(_BF)
            return pt, qt

        nb = PEER_BUILD_TOKENS

        def group(g, _):
            for p in range(SUBLANES):
                oh = [onehots((nb * g + b) * SUBLANES + p) for b in range(nb)]
                lhs = jnp.concatenate([pt for pt, _ in oh], axis=1)
                rhs_t = jnp.concatenate(
                    [jnp.concatenate([oh[b][1] if c == b else zeros for c in range(nb)], axis=1)
                     for b in range(nb)], axis=0)
                w = lax.dot_general(lhs, rhs_t, _NT, preferred_element_type=_F32)
                for b in range(nb):
                    w_scr[nb * g + b, pl.ds(p, PEER_N_KEYS, stride=SUBLANES), :] = (
                        w[:, b * PEER_N_KEYS:(b + 1) * PEER_N_KEYS])
            return 0

        lax.fori_loop(0, PEER_TM // (nb * SUBLANES), group, 0)

    a = lax.dot_general(xn_scr[...], u_ref[...], _NT, preferred_element_type=_F32)
    wa = []
    for il in range(PEER_I1_TILE):
        r0 = pl.multiple_of((j * PEER_I1_TILE + il) * SUBLANES, SUBLANES)
        w = w_scr[:, pl.ds(r0, SUBLANES), :].reshape(PEER_TM, PEER_N_KEYS)
        wa.append((w * jax.nn.gelu(a[:, il * PEER_N_KEYS:(il + 1) * PEER_N_KEYS])).astype(_BF))
    h2_ref[...] += jnp.dot(jnp.concatenate(wa, axis=-1), v_ref[...], preferred_element_type=_F32)


def _peer(eid, gate, h1, g_ffn, u_bf, v_bf):
    T = h1.shape[0]
    n_exp = v_bf.shape[0]
    te = PEER_I1_TILE * PEER_N_KEYS
    return pl.pallas_call(
        _peer_kernel,
        grid=(T // PEER_TM, n_exp // te),
        in_specs=[
            pl.BlockSpec((PEER_TM, PEER_SLOTS), lambda i, j: (i, 0)),
            pl.BlockSpec((PEER_TM, PEER_SLOTS), lambda i, j: (i, 0)),
            pl.BlockSpec((PEER_TM, D_MODEL), lambda i, j: (i, 0), pipeline_mode=pl.Buffered(1)),
            pl.BlockSpec((1, D_MODEL), lambda i, j: (0, 0)),
            pl.BlockSpec((te, D_MODEL), lambda i, j: (j, 0)),
            pl.BlockSpec((te, D_MODEL), lambda i, j: (j, 0)),
        ],
        out_specs=pl.BlockSpec((PEER_TM, D_MODEL), lambda i, j: (i, 0)),
        out_shape=jax.ShapeDtypeStruct((T, D_MODEL), _F32),
        scratch_shapes=[
            pltpu.VMEM((PEER_TM // SUBLANES, PEER_N_KEYS * SUBLANES, PEER_N_KEYS), _F32),
            pltpu.VMEM((PEER_TM, D_MODEL), _BF),
        ],
        compiler_params=pltpu.CompilerParams(
            dimension_semantics=("parallel", "arbitrary"),
            vmem_limit_bytes=V7X_VMEM_LIMIT_BYTES),
        name="peer",
    )(eid, gate, h1, g_ffn, u_bf, v_bf)


def _ple_kernel(final, h_ref, p_ref, g_ref, wg_ref, wp_ref, gf_ref, out_ref):
    h = h_ref[...]
    gate = jax.nn.sigmoid(_mm(_rms(h, g_ref[...]), wg_ref[...]))
    h = h + _mm(p_ref[...], wp_ref[...]) * gate
    out_ref[...] = _rms(h, gf_ref[...]) if final else h


def _ple(h2, p2, g_ple, w_gate_bf, w_ple_bf, g_final, final):
    T = h2.shape[0]
    full = lambda shape: pl.BlockSpec(shape, lambda i: (0,) * len(shape))
    return pl.pallas_call(
        functools.partial(_ple_kernel, final),
        grid=(T // PLE_TM,),
        in_specs=[
            pl.BlockSpec((PLE_TM, D_MODEL), lambda i: (i, 0)),
            pl.BlockSpec((PLE_TM, PLE_DIM), lambda i: (i, 0)),
            full((1, D_MODEL)),
            full((D_MODEL, D_MODEL)),
            full((PLE_DIM, D_MODEL)),
            full((1, D_MODEL)),
        ],
        out_specs=pl.BlockSpec((PLE_TM, D_MODEL), lambda i: (i, 0)),
        out_shape=jax.ShapeDtypeStruct((T, D_MODEL), _F32),
        compiler_params=pltpu.CompilerParams(
            dimension_semantics=("parallel",), vmem_limit_bytes=V7X_VMEM_LIMIT_BYTES),
        name="ple",
    )(h2, p2, g_ple, w_gate_bf, w_ple_bf, g_final)


def kernel(x, p, g_mix, w_in, gmlp_ln_g, gmlp_ln_b, gmlp_w_s, gmlp_b_s, gmlp_beta, lambda_q1,
           lambda_k1, lambda_q2, lambda_k2, subln_g, rel_bias, w_out, g_ffn, peer_w_q, peer_keys,
           peer_u, peer_v, g_ple, w_ple, w_gate, g_final):
    B, S, D = x.shape
    depth = w_in.shape[0]
    T = B * S
    row = lambda a: a.reshape(1, -1)
    h = x.reshape(T, D)
    bias_tiles = _relbias(rel_bias)
    for i in range(depth):
        lam_init = 0.8 - 0.6 * math.exp(-0.3 * i)
        out_a, qk, vt = _mix_in(h, S, row(g_mix[i]), w_in[i].astype(_BF), row(gmlp_ln_g[i]),
                                row(gmlp_ln_b[i]), gmlp_w_s[i], jnp.transpose(gmlp_b_s[i]),
                                row(gmlp_beta[i]))
        out_b = _diffattn(qk.reshape(B, S, QK_COLS), vt, bias_tiles, row(lambda_q1[i]),
                          row(lambda_k1[i]), row(lambda_q2[i]), row(lambda_k2[i]),
                          row(subln_g[i]), lam_init)
        h1, eid, gate = _route(h, out_a, out_b.reshape(T, DIFF_WIDTH), w_out[i].astype(_BF),
                               row(g_ffn[i]), peer_w_q[i].astype(_BF), peer_keys[i].astype(_BF))
        h2 = _peer(eid, gate, h1, row(g_ffn[i]), peer_u[i].astype(_BF),
                   peer_v[i].astype(_BF))
        h = _ple(h2, p[i].reshape(T, PLE_DIM), row(g_ple[i]), w_gate[i].astype(_BF),
                 w_ple[i].astype(_BF), row(g_final), i == depth - 1)
    return h.reshape(B, S, D)
```

```python
import functools
import math

import jax
import jax.numpy as jnp
from jax import lax
from jax.experimental import pallas as pl
from jax.experimental.pallas import tpu as pltpu

D_MODEL = 1024
CHUNK = 64
GMLP_WIDTH = 512
GMLP_GROUPS = 4
GMLP_GROUP_CH = 128
GMLP_BLOCK = 128
DIFF_HEADS = 4
DIFF_HEAD_DIM = 64
DIFF_V_DIM = 128
DIFF_WIDTH = 512
A_COLS = 1024
Q_COLS = 512
V_COLS = 512
IN_COLS = A_COLS + 2 * Q_COLS + V_COLS
REL_BUCKETS = 32
REL_MAX_EXACT = 8
PEER_N_KEYS = 128
PEER_HEADS = 8
PEER_TOPK = 16
PEER_HALF = 128
PEER_DKEY = 256
PEER_SLOTS = PEER_HEADS * PEER_TOPK
PLE_DIM = 256
EPS = 1e-6
NEG_INF = -1e30
SUBLANES = 8

V7X_VMEM_LIMIT_BYTES = 56 * 1024 * 1024

MIX_TM = 256
ATT_T = 512
RELBIAS_ROWS = 128
ROUTE_TM = 256
PEER_TM = 512
PEER_I1_TILE = 8
PEER_BUILD_TOKENS = 2
PEER_BUILD_UNROLL = 4
PLE_TM = 512

_NT = (((1,), (1,)), ((), ()))
_BF = jnp.bfloat16
_F32 = jnp.float32


def _rms(x, g):
    return x * lax.rsqrt(jnp.mean(x * x, axis=-1, keepdims=True) + EPS) * g


def _mm(a, b):
    return jnp.dot(a.astype(_BF), b.astype(_BF), preferred_element_type=_F32)


def _mm_nt(a, b):
    return lax.dot_general(a.astype(_BF), b.astype(_BF), _NT, preferred_element_type=_F32)


def _mix_in_kernel(x_ref, g_ref, w_ref, lng_ref, lnb_ref, ws_ref, bst_ref, beta_ref,
                   outa_ref, k_ref, qt_ref, vt_ref):
    n1 = _rms(x_ref[...], g_ref[...])
    z = jnp.dot(n1.astype(_BF), w_ref[...], preferred_element_type=_F32)
    qt_ref[...] = z[:, A_COLS:A_COLS + Q_COLS].T.astype(_BF)
    k_ref[...] = z[:, A_COLS + Q_COLS:A_COLS + 2 * Q_COLS].astype(_BF)
    vt_ref[...] = z[:, A_COLS + 2 * Q_COLS:].T.astype(_BF)
    za = jax.nn.gelu(z[:, :A_COLS])
    pos_i = lax.broadcasted_iota(jnp.int32, (GMLP_BLOCK, GMLP_BLOCK), 0) // CHUNK
    pos_j = lax.broadcasted_iota(jnp.int32, (GMLP_BLOCK, GMLP_BLOCK), 1) // CHUNK
    causal = pos_j <= pos_i
    for g in range(GMLP_GROUPS):
        c0 = g * GMLP_GROUP_CH
        u = za[:, c0:c0 + GMLP_GROUP_CH]
        v = za[:, GMLP_WIDTH + c0:GMLP_WIDTH + c0 + GMLP_GROUP_CH]
        mu = jnp.mean(v, axis=-1, keepdims=True)
        vc = v - mu
        vn = vc * lax.rsqrt(jnp.mean(vc * vc, axis=-1, keepdims=True) + EPS)
        vn = vn * lng_ref[:, c0:c0 + GMLP_GROUP_CH] + lnb_ref[:, c0:c0 + GMLP_GROUP_CH]
        wm = jnp.where(causal, ws_ref[g], 0.0)
        for n in range(MIX_TM // GMLP_BLOCK):
            r0 = n * GMLP_BLOCK
            sv = _mm(wm, vn[r0:r0 + GMLP_BLOCK]) + bst_ref[:, g:g + 1]
            o = u[r0:r0 + GMLP_BLOCK] * sv
            outa_ref[r0:r0 + GMLP_BLOCK, c0:c0 + GMLP_GROUP_CH] = _rms(
                o, beta_ref[:, c0:c0 + GMLP_GROUP_CH])


def _mix_in(x2, seq, g_mix, w_in_bf, ln_g, ln_b, w_s, b_s_t, beta):
    T = x2.shape[0]
    n_seq = seq // MIX_TM
    full = lambda shape: pl.BlockSpec(shape, lambda i: (0,) * len(shape))
    return pl.pallas_call(
        _mix_in_kernel,
        grid=(T // MIX_TM,),
        in_specs=[
            pl.BlockSpec((MIX_TM, D_MODEL), lambda i: (i, 0)),
            full((1, D_MODEL)),
            full((D_MODEL, IN_COLS)),
            full((1, GMLP_WIDTH)),
            full((1, GMLP_WIDTH)),
            full((GMLP_GROUPS, GMLP_BLOCK, GMLP_BLOCK)),
            full((GMLP_BLOCK, GMLP_GROUPS)),
            full((1, GMLP_WIDTH)),
        ],
        out_specs=[
            pl.BlockSpec((MIX_TM, GMLP_WIDTH), lambda i: (i, 0)),
            pl.BlockSpec((MIX_TM, Q_COLS), lambda i: (i, 0)),
            pl.BlockSpec((None, Q_COLS, MIX_TM), lambda i: (i // n_seq, 0, i % n_seq)),
            pl.BlockSpec((None, V_COLS, MIX_TM), lambda i: (i // n_seq, 0, i % n_seq)),
        ],
        out_shape=[
            jax.ShapeDtypeStruct((T, GMLP_WIDTH), _F32),
            jax.ShapeDtypeStruct((T, Q_COLS), _BF),
            jax.ShapeDtypeStruct((T // seq, Q_COLS, seq), _BF),
            jax.ShapeDtypeStruct((T // seq, V_COLS, seq), _BF),
        ],
        compiler_params=pltpu.CompilerParams(
            dimension_semantics=("parallel",), vmem_limit_bytes=V7X_VMEM_LIMIT_BYTES),
        name="mix_in",
    )(x2, g_mix, w_in_bf, ln_g, ln_b, w_s, b_s_t, beta)


def _relbias_kernel(rb_ref, out_ref):
    h, d, r = pl.program_id(0), pl.program_id(1), pl.program_id(2)
    kj = lax.broadcasted_iota(jnp.int32, (RELBIAS_ROWS, ATT_T), 0) + r * RELBIAS_ROWS
    qi = lax.broadcasted_iota(jnp.int32, (RELBIAS_ROWS, ATT_T), 1)
    rel = kj - qi - d * ATT_T
    n = jnp.abs(rel)
    n2 = n * n
    large = jnp.full_like(n, REL_MAX_EXACT)
    for k in range(1, 8):
        large = large + (n2 >= (1 << (6 + k))).astype(jnp.int32)
    bucket = jnp.where(rel > 0, REL_BUCKETS // 2, 0) + jnp.where(n < REL_MAX_EXACT, n, large)
    bias = jnp.zeros(rel.shape, _F32)
    for b in range(REL_BUCKETS):
        bias = jnp.where(bucket == b, rb_ref[b, h], bias)
    bias = bias - rb_ref[REL_BUCKETS // 2 - 1, h]
    visible = ((kj - d * ATT_T) // CHUNK) <= (qi // CHUNK)
    out_ref[...] = jnp.where(visible, bias, NEG_INF)


def _relbias(rel_bias):
    return pl.pallas_call(
        _relbias_kernel,
        grid=(DIFF_HEADS, 2, ATT_T // RELBIAS_ROWS),
        in_specs=[pl.BlockSpec(memory_space=pltpu.SMEM)],
        out_specs=pl.BlockSpec((None, None, RELBIAS_ROWS, ATT_T), lambda h, d, r: (h, d, r, 0)),
        out_shape=jax.ShapeDtypeStruct((DIFF_HEADS, 2, ATT_T, ATT_T), _F32),
        name="relbias",
    )(rel_bias)


def _diffattn_kernel(lam_init, qt_ref, k_ref, vt_ref, bias_ref, lq1_ref, lk1_ref, lq2_ref, lk2_ref,
                     sg_ref, out_ref, m_scr, l_scr, acc_scr):
    qi = pl.program_id(2)
    qt = qt_ref[...] * (DIFF_HEAD_DIM ** -0.5)
    chan = lax.broadcasted_iota(jnp.int32, qt.shape, 0)
    zero = jnp.zeros_like(qt)
    q2x = jnp.concatenate([jnp.where(chan < DIFF_HEAD_DIM, qt, zero),
                           jnp.where(chan >= DIFF_HEAD_DIM, qt, zero)], axis=1)

    def step(j, bias, first):
        r0 = pl.multiple_of(j * ATT_T, ATT_T)
        kt = k_ref[pl.ds(r0, ATT_T), :]
        vt = vt_ref[:, pl.ds(r0, ATT_T)]
        s = jnp.dot(kt, q2x, preferred_element_type=_F32)
        if bias is not None:
            s = s + jnp.concatenate([bias, bias], axis=1)
        s_max = jnp.max(s, axis=0, keepdims=True)
        if first:
            m_new = s_max
            p = jnp.exp(s - m_new)
            l_scr[...] = jnp.sum(p, axis=0, keepdims=True)
            acc_scr[...] = jnp.dot(vt, p.astype(_BF), preferred_element_type=_F32)
        else:
            m_old = m_scr[...]
            m_new = jnp.maximum(m_old, s_max)
            a = jnp.exp(m_old - m_new)
            p = jnp.exp(s - m_new)
            l_scr[...] = a * l_scr[...] + jnp.sum(p, axis=0, keepdims=True)
            acc_scr[...] = a * acc_scr[...] + jnp.dot(vt, p.astype(_BF),
                                                      preferred_element_type=_F32)
        m_scr[...] = m_new

    step(qi, bias_ref[0], True)

    @pl.when(qi > 0)
    def _():
        step(qi - 1, bias_ref[1], False)

    def far(j, _):
        step(j, None, False)
        return 0

    lax.fori_loop(0, qi - 1, far, 0)

    lam = (jnp.exp(jnp.sum(lq1_ref[...] * lk1_ref[...], axis=-1, keepdims=True))
           - jnp.exp(jnp.sum(lq2_ref[...] * lk2_ref[...], axis=-1, keepdims=True)) + lam_init)
    o = acc_scr[...] / l_scr[...]
    o = o[:, :ATT_T] - lam * o[:, ATT_T:]
    o = o * lax.rsqrt(jnp.mean(o * o, axis=0, keepdims=True) + EPS)
    out_ref[...] = o.T * sg_ref[...] * (1.0 - lam_init)


def _diffattn(k3, qt3, vt3, bias_tiles, lq1, lk1, lq2, lk2, subln_g, lam_init):
    B, S, _ = k3.shape
    nq = S // ATT_T
    vec = lambda n: pl.BlockSpec((1, n), lambda b, h, i: (0, 0))
    return pl.pallas_call(
        functools.partial(_diffattn_kernel, lam_init),
        grid=(B, DIFF_HEADS, nq),
        in_specs=[
            pl.BlockSpec((None, 2 * DIFF_HEAD_DIM, ATT_T), lambda b, h, i: (b, h, i)),
            pl.BlockSpec((None, S, 2 * DIFF_HEAD_DIM), lambda b, h, i: (b, 0, h)),
            pl.BlockSpec((None, DIFF_V_DIM, S), lambda b, h, i: (b, h, 0)),
            pl.BlockSpec((None, 2, ATT_T, ATT_T), lambda b, h, i: (h, 0, 0, 0)),
            vec(DIFF_HEAD_DIM), vec(DIFF_HEAD_DIM), vec(DIFF_HEAD_DIM), vec(DIFF_HEAD_DIM),
            vec(DIFF_V_DIM),
        ],
        out_specs=pl.BlockSpec((None, ATT_T, DIFF_V_DIM), lambda b, h, i: (b, i, h)),
        out_shape=jax.ShapeDtypeStruct((B, S, DIFF_WIDTH), _F32),
        scratch_shapes=[
            pltpu.VMEM((1, 2 * ATT_T), _F32),
            pltpu.VMEM((1, 2 * ATT_T), _F32),
            pltpu.VMEM((DIFF_V_DIM, 2 * ATT_T), _F32),
        ],
        compiler_params=pltpu.CompilerParams(
            dimension_semantics=("parallel", "parallel", "arbitrary"),
            vmem_limit_bytes=V7X_VMEM_LIMIT_BYTES),
        name="diffattn",
    )(qt3, k3, vt3, bias_tiles, lq1, lk1, lq2, lk2, subln_g)


def _top16(s, fill):
    n_rows = s.shape[0]
    iota = lax.broadcasted_iota(jnp.int32, s.shape, 0)
    vals, poss = [], []
    for _ in range(PEER_TOPK):
        m = jnp.max(s, axis=0, keepdims=True)
        pos = jnp.min(jnp.where(s == m, iota, n_rows), axis=0, keepdims=True)
        vals.append(m)
        poss.append(pos)
        s = jnp.where(iota == pos, fill, s)
    return vals, poss


def _candidates(v1, i1, v2, i2):
    neg = -jnp.inf
    v1c, i1c = jnp.concatenate(v1, axis=0), jnp.concatenate(i1, axis=0)
    v2c, i2c = jnp.concatenate(v2, axis=0), jnp.concatenate(i2, axis=0)
    row = lax.broadcasted_iota(jnp.int32, (SUBLANES, v1c.shape[1]), 0)
    cand = [v1[0] + v2c]
    cidx = [i1[0] * PEER_N_KEYS + i2c]
    for a in range(1, SUBLANES):
        nb = PEER_TOPK // (a + 1)
        cand.append(jnp.where(row < nb, v1[a] + v2c[:SUBLANES], neg))
        cidx.append(i1[a] * PEER_N_KEYS + i2c[:SUBLANES])
    cand.append(v1c[SUBLANES:] + v2[0])
    cidx.append(i1c[SUBLANES:] * PEER_N_KEYS + i2[0])
    return jnp.concatenate(cand, axis=0), jnp.concatenate(cidx, axis=0)


def _route_kernel(x_ref, oa_ref, ob_ref, wo_ref, g_ref, wq_ref, keys_ref,
                  h1_ref, eid_ref, gate_ref, qp_scr, eid_scr, gate_scr):
    mix = jnp.concatenate([oa_ref[...], ob_ref[...]], axis=-1)
    h1 = x_ref[...] + _mm(mix, wo_ref[...])
    h1_ref[...] = h1
    xn = _rms(h1, g_ref[...])
    qp_scr[...] = _mm(xn, wq_ref[...])
    neg = -jnp.inf

    def head(h, _):
        c0 = pl.multiple_of(h * PEER_DKEY, PEER_DKEY)
        q1 = qp_scr[:, pl.ds(c0, PEER_HALF)]
        q2 = qp_scr[:, pl.ds(c0 + PEER_HALF, PEER_HALF)]
        v1, i1 = _top16(_mm_nt(keys_ref[h, 0], q1), neg)
        v2, i2 = _top16(_mm_nt(keys_ref[h, 1], q2), neg)
        cand, cidx = _candidates(v1, i1, v2, i2)
        best, pos = _top16(cand, neg)
        iota = lax.broadcasted_iota(jnp.int32, cand.shape, 0)
        eid = jnp.concatenate(
            [jnp.max(jnp.where(iota == p, cidx, -1), axis=0, keepdims=True) for p in pos], axis=0)
        best = jnp.concatenate(best, axis=0)
        e = jnp.exp(best - best[0:1])
        gate = e / jnp.sum(e, axis=0, keepdims=True)
        r0 = pl.multiple_of(h * PEER_TOPK, PEER_TOPK)
        eid_scr[pl.ds(r0, PEER_TOPK), :] = eid
        gate_scr[pl.ds(r0, PEER_TOPK), :] = gate
        return 0

    lax.fori_loop(0, PEER_HEADS, head, 0)
    eid_ref[...] = eid_scr[...].T
    gate_ref[...] = gate_scr[...].T


def _route(x2, out_a, out_b, w_out_bf, g_ffn, w_q_bf, keys_bf):
    T = x2.shape[0]
    tm = ROUTE_TM
    full = lambda shape: pl.BlockSpec(shape, lambda i: (0,) * len(shape))
    return pl.pallas_call(
        _route_kernel,
        grid=(T // tm,),
        in_specs=[
            pl.BlockSpec((tm, D_MODEL), lambda i: (i, 0)),
            pl.BlockSpec((tm, GMLP_WIDTH), lambda i: (i, 0)),
            pl.BlockSpec((tm, DIFF_WIDTH), lambda i: (i, 0)),
            full((D_MODEL, D_MODEL)),
            full((1, D_MODEL)),
            full((D_MODEL, PEER_HEADS * PEER_DKEY)),
            full((PEER_HEADS, 2, PEER_N_KEYS, PEER_HALF)),
        ],
        out_specs=[
            pl.BlockSpec((tm, D_MODEL), lambda i: (i, 0)),
            pl.BlockSpec((tm, PEER_SLOTS), lambda i: (i, 0)),
            pl.BlockSpec((tm, PEER_SLOTS), lambda i: (i, 0)),
        ],
        out_shape=[
            jax.ShapeDtypeStruct((T, D_MODEL), _F32),
            jax.ShapeDtypeStruct((T, PEER_SLOTS), jnp.int32),
            jax.ShapeDtypeStruct((T, PEER_SLOTS), _F32),
        ],
        scratch_shapes=[
            pltpu.VMEM((tm, PEER_HEADS * PEER_DKEY), _F32),
            pltpu.VMEM((PEER_SLOTS, tm), jnp.int32),
            pltpu.VMEM((PEER_SLOTS, tm), _F32),
        ],
        compiler_params=pltpu.CompilerParams(
            dimension_semantics=("parallel",), vmem_limit_bytes=V7X_VMEM_LIMIT_BYTES),
        name="route",
    )(x2, out_a, out_b, w_out_bf, g_ffn, w_q_bf, keys_bf)


def _peer_kernel(eid_ref, gate_ref, h1_ref, g_ref, ut_ref, v_ref, h2_ref, w_scr, xn_scr):
    j = pl.program_id(1)

    @pl.when(j == 0)
    def _():
        h1 = h1_ref[...]
        xn_scr[...] = _rms(h1, g_ref[...]).astype(_BF)
        h2_ref[...] = h1
        iota = lax.broadcasted_iota(jnp.int32, (PEER_N_KEYS, PEER_SLOTS), 0)
        zeros = jnp.zeros((PEER_N_KEYS, PEER_SLOTS), _BF)

        def onehots(t):
            e_row = eid_ref[pl.ds(t, 1), :]
            pt = jnp.where(iota == (e_row >> 7), gate_ref[pl.ds(t, 1), :], 0.0).astype(_BF)
            qt = jnp.where(iota == (e_row & (PEER_N_KEYS - 1)), 1.0, 0.0).astype(_BF)
            return pt, qt

        nb = PEER_BUILD_TOKENS

        def build(g0, p):
            oh = [onehots((g0 + b) * SUBLANES + p) for b in range(nb)]
            lhs = jnp.concatenate([pt for pt, _ in oh], axis=1)
            rhs_t = jnp.concatenate(
                [jnp.concatenate([oh[b][1] if c == b else zeros for c in range(nb)], axis=1)
                 for b in range(nb)], axis=0)
            w = lax.dot_general(lhs, rhs_t, _NT, preferred_element_type=_F32)
            for b in range(nb):
                w_scr[g0 + b, pl.ds(p, PEER_N_KEYS, stride=SUBLANES), :] = (
                    w[:, b * PEER_N_KEYS:(b + 1) * PEER_N_KEYS])

        def groups(g, _):
            for u in range(PEER_BUILD_UNROLL):
                for p in range(SUBLANES):
                    build((g * PEER_BUILD_UNROLL + u) * nb, p)
            return 0

        lax.fori_loop(0, PEER_TM // (PEER_BUILD_UNROLL * nb * SUBLANES), groups, 0)

    a = jnp.dot(xn_scr[...], ut_ref[...], preferred_element_type=_F32)
    wa = []
    for il in range(PEER_I1_TILE):
        r0 = pl.multiple_of((j * PEER_I1_TILE + il) * SUBLANES, SUBLANES)
        w = w_scr[:, pl.ds(r0, SUBLANES), :].reshape(PEER_TM, PEER_N_KEYS)
        wa.append((w * jax.nn.gelu(a[:, il * PEER_N_KEYS:(il + 1) * PEER_N_KEYS])).astype(_BF))
    h2_ref[...] += jnp.dot(jnp.concatenate(wa, axis=-1), v_ref[...], preferred_element_type=_F32)


def _peer(eid, gate, h1, g_ffn, u_t_bf, v_bf):
    T = h1.shape[0]
    n_exp = v_bf.shape[0]
    te = PEER_I1_TILE * PEER_N_KEYS
    return pl.pallas_call(
        _peer_kernel,
        grid=(T // PEER_TM, n_exp // te),
        in_specs=[
            pl.BlockSpec((PEER_TM, PEER_SLOTS), lambda i, j: (i, 0)),
            pl.BlockSpec((PEER_TM, PEER_SLOTS), lambda i, j: (i, 0)),
            pl.BlockSpec((PEER_TM, D_MODEL), lambda i, j: (i, 0), pipeline_mode=pl.Buffered(1)),
            pl.BlockSpec((1, D_MODEL), lambda i, j: (0, 0)),
            pl.BlockSpec((D_MODEL, te), lambda i, j: (0, j)),
            pl.BlockSpec((te, D_MODEL), lambda i, j: (j, 0)),
        ],
        out_specs=pl.BlockSpec((PEER_TM, D_MODEL), lambda i, j: (i, 0)),
        out_shape=jax.ShapeDtypeStruct((T, D_MODEL), _F32),
        scratch_shapes=[
            pltpu.VMEM((PEER_TM // SUBLANES, PEER_N_KEYS * SUBLANES, PEER_N_KEYS), _F32),
            pltpu.VMEM((PEER_TM, D_MODEL), _BF),
        ],
        compiler_params=pltpu.CompilerParams(
            dimension_semantics=("parallel", "arbitrary"),
            vmem_limit_bytes=V7X_VMEM_LIMIT_BYTES),
        name="peer",
    )(eid, gate, h1, g_ffn, u_t_bf, v_bf)


def _ple_kernel(final, h_ref, p_ref, g_ref, wg_ref, wp_ref, gf_ref, out_ref):
    h = h_ref[...]
    gate = jax.nn.sigmoid(_mm(_rms(h, g_ref[...]), wg_ref[...]))
    h = h + _mm(p_ref[...], wp_ref[...]) * gate
    out_ref[...] = _rms(h, gf_ref[...]) if final else h


def _ple(h2, p2, g_ple, w_gate_bf, w_ple_bf, g_final, final):
    T = h2.shape[0]
    full = lambda shape: pl.BlockSpec(shape, lambda i: (0,) * len(shape))
    return pl.pallas_call(
        functools.partial(_ple_kernel, final),
        grid=(T // PLE_TM,),
        in_specs=[
            pl.BlockSpec((PLE_TM, D_MODEL), lambda i: (i, 0)),
            pl.BlockSpec((PLE_TM, PLE_DIM), lambda i: (i, 0)),
            full((1, D_MODEL)),
            full((D_MODEL, D_MODEL)),
            full((PLE_DIM, D_MODEL)),
            full((1, D_MODEL)),
        ],
        out_specs=pl.BlockSpec((PLE_TM, D_MODEL), lambda i: (i, 0)),
        out_shape=jax.ShapeDtypeStruct((T, D_MODEL), _F32),
        compiler_params=pltpu.CompilerParams(
            dimension_semantics=("parallel",), vmem_limit_bytes=V7X_VMEM_LIMIT_BYTES),
        name="ple",
    )(h2, p2, g_ple, w_gate_bf, w_ple_bf, g_final)


def kernel(x, p, g_mix, w_in, gmlp_ln_g, gmlp_ln_b, gmlp_w_s, gmlp_b_s, gmlp_beta, lambda_q1,
           lambda_k1, lambda_q2, lambda_k2, subln_g, rel_bias, w_out, g_ffn, peer_w_q, peer_keys,
           peer_u, peer_v, g_ple, w_ple, w_gate, g_final):
    B, S, D = x.shape
    depth = w_in.shape[0]
    T = B * S
    row = lambda a: a.reshape(1, -1)
    h = x.reshape(T, D)
    bias_tiles = _relbias(rel_bias)
    for i in range(depth):
        lam_init = 0.8 - 0.6 * math.exp(-0.3 * i)
        out_a, k, qt, vt = _mix_in(h, S, row(g_mix[i]), w_in[i].astype(_BF), row(gmlp_ln_g[i]),
                                row(gmlp_ln_b[i]), gmlp_w_s[i], jnp.transpose(gmlp_b_s[i]),
                                row(gmlp_beta[i]))
        out_b = _diffattn(k.reshape(B, S, Q_COLS), qt, vt, bias_tiles, row(lambda_q1[i]),
                          row(lambda_k1[i]), row(lambda_q2[i]), row(lambda_k2[i]),
                          row(subln_g[i]), lam_init)
        h1, eid, gate = _route(h, out_a, out_b.reshape(T, DIFF_WIDTH), w_out[i].astype(_BF),
                               row(g_ffn[i]), peer_w_q[i].astype(_BF), peer_keys[i].astype(_BF))
        h2 = _peer(eid, gate, h1, row(g_ffn[i]), jnp.transpose(peer_u[i]).astype(_BF),
                   peer_v[i].astype(_BF))
        h = _ple(h2, p[i].reshape(T, PLE_DIM), row(g_ple[i]), w_gate[i].astype(_BF),
                 w_ple[i].astype(_BF), row(g_final), i == depth - 1)
    return h.reshape(B, S, D)
```

```python
import functools
import math

import jax
import jax.numpy as jnp
from jax import lax
from jax.experimental import pallas as pl
from jax.experimental.pallas import tpu as pltpu

D_MODEL = 1024
CHUNK = 64
GMLP_WIDTH = 512
GMLP_GROUPS = 4
GMLP_GROUP_CH = 128
GMLP_BLOCK = 128
DIFF_HEADS = 4
DIFF_HEAD_DIM = 64
DIFF_V_DIM = 128
DIFF_WIDTH = 512
A_COLS = 1024
Q_COLS = 512
V_COLS = 512
IN_COLS = A_COLS + 2 * Q_COLS + V_COLS
REL_BUCKETS = 32
REL_MAX_EXACT = 8
PEER_N_KEYS = 128
PEER_HEADS = 8
PEER_TOPK = 16
PEER_HALF = 128
PEER_DKEY = 256
PEER_SLOTS = PEER_HEADS * PEER_TOPK
PLE_DIM = 256
EPS = 1e-6
NEG_INF = -1e30
SUBLANES = 8

V7X_VMEM_LIMIT_BYTES = 56 * 1024 * 1024

MIX_TM = 256
ATT_T = 512
RELBIAS_ROWS = 128
ROUTE_TM = 256
ROUTE_HEAD_UNROLL = 4
PEER_TM = 512
PEER_I1_TILE = 8
PEER_BUILD_TOKENS = 2
PEER_BUILD_UNROLL = 4
PLE_TM = 512

_NT = (((1,), (1,)), ((), ()))
_BF = jnp.bfloat16
_F32 = jnp.float32


def _rms(x, g):
    return x * lax.rsqrt(jnp.mean(x * x, axis=-1, keepdims=True) + EPS) * g


def _mm(a, b):
    return jnp.dot(a.astype(_BF), b.astype(_BF), preferred_element_type=_F32)


def _mm_nt(a, b):
    return lax.dot_general(a.astype(_BF), b.astype(_BF), _NT, preferred_element_type=_F32)


def _mix_in_kernel(x_ref, g_ref, w_ref, lng_ref, lnb_ref, ws_ref, bst_ref, beta_ref,
                   outa_ref, k_ref, qt_ref, vt_ref):
    n1 = _rms(x_ref[...], g_ref[...])
    z = jnp.dot(n1.astype(_BF), w_ref[...], preferred_element_type=_F32)
    qt_ref[...] = z[:, A_COLS:A_COLS + Q_COLS].T.astype(_BF)
    k_ref[...] = z[:, A_COLS + Q_COLS:A_COLS + 2 * Q_COLS].astype(_BF)
    vt_ref[...] = z[:, A_COLS + 2 * Q_COLS:].T.astype(_BF)
    za = jax.nn.gelu(z[:, :A_COLS])
    pos_i = lax.broadcasted_iota(jnp.int32, (GMLP_BLOCK, GMLP_BLOCK), 0) // CHUNK
    pos_j = lax.broadcasted_iota(jnp.int32, (GMLP_BLOCK, GMLP_BLOCK), 1) // CHUNK
    causal = pos_j <= pos_i
    for g in range(GMLP_GROUPS):
        c0 = g * GMLP_GROUP_CH
        u = za[:, c0:c0 + GMLP_GROUP_CH]
        v = za[:, GMLP_WIDTH + c0:GMLP_WIDTH + c0 + GMLP_GROUP_CH]
        mu = jnp.mean(v, axis=-1, keepdims=True)
        vc = v - mu
        vn = vc * lax.rsqrt(jnp.mean(vc * vc, axis=-1, keepdims=True) + EPS)
        vn = vn * lng_ref[:, c0:c0 + GMLP_GROUP_CH] + lnb_ref[:, c0:c0 + GMLP_GROUP_CH]
        wm = jnp.where(causal, ws_ref[g], 0.0)
        for n in range(MIX_TM // GMLP_BLOCK):
            r0 = n * GMLP_BLOCK
            sv = _mm(wm, vn[r0:r0 + GMLP_BLOCK]) + bst_ref[:, g:g + 1]
            o = u[r0:r0 + GMLP_BLOCK] * sv
            outa_ref[r0:r0 + GMLP_BLOCK, c0:c0 + GMLP_GROUP_CH] = _rms(
                o, beta_ref[:, c0:c0 + GMLP_GROUP_CH])


def _mix_in(x2, seq, g_mix, w_in_bf, ln_g, ln_b, w_s, b_s_t, beta):
    T = x2.shape[0]
    n_seq = seq // MIX_TM
    full = lambda shape: pl.BlockSpec(shape, lambda i: (0,) * len(shape))
    return pl.pallas_call(
        _mix_in_kernel,
        grid=(T // MIX_TM,),
        in_specs=[
            pl.BlockSpec((MIX_TM, D_MODEL), lambda i: (i, 0)),
            full((1, D_MODEL)),
            full((D_MODEL, IN_COLS)),
            full((1, GMLP_WIDTH)),
            full((1, GMLP_WIDTH)),
            full((GMLP_GROUPS, GMLP_BLOCK, GMLP_BLOCK)),
            full((GMLP_BLOCK, GMLP_GROUPS)),
            full((1, GMLP_WIDTH)),
        ],
        out_specs=[
            pl.BlockSpec((MIX_TM, GMLP_WIDTH), lambda i: (i, 0)),
            pl.BlockSpec((MIX_TM, Q_COLS), lambda i: (i, 0)),
            pl.BlockSpec((None, Q_COLS, MIX_TM), lambda i: (i // n_seq, 0, i % n_seq)),
            pl.BlockSpec((None, V_COLS, MIX_TM), lambda i: (i // n_seq, 0, i % n_seq)),
        ],
        out_shape=[
            jax.ShapeDtypeStruct((T, GMLP_WIDTH), _F32),
            jax.ShapeDtypeStruct((T, Q_COLS), _BF),
            jax.ShapeDtypeStruct((T // seq, Q_COLS, seq), _BF),
            jax.ShapeDtypeStruct((T // seq, V_COLS, seq), _BF),
        ],
        compiler_params=pltpu.CompilerParams(
            dimension_semantics=("parallel",), vmem_limit_bytes=V7X_VMEM_LIMIT_BYTES),
        name="mix_in",
    )(x2, g_mix, w_in_bf, ln_g, ln_b, w_s, b_s_t, beta)


def _relbias_kernel(rb_ref, out_ref):
    h, d, r = pl.program_id(0), pl.program_id(1), pl.program_id(2)
    kj = lax.broadcasted_iota(jnp.int32, (RELBIAS_ROWS, ATT_T), 0) + r * RELBIAS_ROWS
    qi = lax.broadcasted_iota(jnp.int32, (RELBIAS_ROWS, ATT_T), 1)
    rel = kj - qi - d * ATT_T
    n = jnp.abs(rel)
    n2 = n * n
    large = jnp.full_like(n, REL_MAX_EXACT)
    for k in range(1, 8):
        large = large + (n2 >= (1 << (6 + k))).astype(jnp.int32)
    bucket = jnp.where(rel > 0, REL_BUCKETS // 2, 0) + jnp.where(n < REL_MAX_EXACT, n, large)
    bias = jnp.zeros(rel.shape, _F32)
    for b in range(REL_BUCKETS):
        bias = jnp.where(bucket == b, rb_ref[b, h], bias)
    bias = bias - rb_ref[REL_BUCKETS // 2 - 1, h]
    visible = ((kj - d * ATT_T) // CHUNK) <= (qi // CHUNK)
    out_ref[...] = jnp.where(visible, bias, NEG_INF)


def _relbias(rel_bias):
    return pl.pallas_call(
        _relbias_kernel,
        grid=(DIFF_HEADS, 2, ATT_T // RELBIAS_ROWS),
        in_specs=[pl.BlockSpec(memory_space=pltpu.SMEM)],
        out_specs=pl.BlockSpec((None, None, RELBIAS_ROWS, ATT_T), lambda h, d, r: (h, d, r, 0)),
        out_shape=jax.ShapeDtypeStruct((DIFF_HEADS, 2, ATT_T, ATT_T), _F32),
        name="relbias",
    )(rel_bias)


def _diffattn_kernel(lam_init, qt_ref, k_ref, vt_ref, bias_ref, lq1_ref, lk1_ref, lq2_ref, lk2_ref,
                     sg_ref, out_ref, m_scr, l_scr, acc_scr):
    qi = pl.program_id(2)
    qt = qt_ref[...] * (DIFF_HEAD_DIM ** -0.5)
    chan = lax.broadcasted_iota(jnp.int32, qt.shape, 0)
    zero = jnp.zeros_like(qt)
    q2x = jnp.concatenate([jnp.where(chan < DIFF_HEAD_DIM, qt, zero),
                           jnp.where(chan >= DIFF_HEAD_DIM, qt, zero)], axis=1)

    def step(j, bias, first):
        r0 = pl.multiple_of(j * ATT_T, ATT_T)
        kt = k_ref[pl.ds(r0, ATT_T), :]
        vt = vt_ref[:, pl.ds(r0, ATT_T)]
        s = jnp.dot(kt, q2x, preferred_element_type=_F32)
        if bias is not None:
            s = s + jnp.concatenate([bias, bias], axis=1)
        s_max = jnp.max(s, axis=0, keepdims=True)
        if first:
            m_new = s_max
            p = jnp.exp(s - m_new)
            l_scr[...] = jnp.sum(p, axis=0, keepdims=True)
            acc_scr[...] = jnp.dot(vt, p.astype(_BF), preferred_element_type=_F32)
        else:
            m_old = m_scr[...]
            m_new = jnp.maximum(m_old, s_max)
            a = jnp.exp(m_old - m_new)
            p = jnp.exp(s - m_new)
            l_scr[...] = a * l_scr[...] + jnp.sum(p, axis=0, keepdims=True)
            acc_scr[...] = a * acc_scr[...] + jnp.dot(vt, p.astype(_BF),
                                                      preferred_element_type=_F32)
        m_scr[...] = m_new

    step(qi, bias_ref[0], True)

    @pl.when(qi > 0)
    def _():
        step(qi - 1, bias_ref[1], False)

    def far(j, _):
        step(j, None, False)
        return 0

    lax.fori_loop(0, qi - 1, far, 0)

    lam = (jnp.exp(jnp.sum(lq1_ref[...] * lk1_ref[...], axis=-1, keepdims=True))
           - jnp.exp(jnp.sum(lq2_ref[...] * lk2_ref[...], axis=-1, keepdims=True)) + lam_init)
    o = acc_scr[...] / l_scr[...]
    o = o[:, :ATT_T] - lam * o[:, ATT_T:]
    o = o * lax.rsqrt(jnp.mean(o * o, axis=0, keepdims=True) + EPS)
    out_ref[...] = o.T * sg_ref[...] * (1.0 - lam_init)


def _diffattn(k3, qt3, vt3, bias_tiles, lq1, lk1, lq2, lk2, subln_g, lam_init):
    B, S, _ = k3.shape
    nq = S // ATT_T
    vec = lambda n: pl.BlockSpec((1, n), lambda b, h, i: (0, 0))
    return pl.pallas_call(
        functools.partial(_diffattn_kernel, lam_init),
        grid=(B, DIFF_HEADS, nq),
        in_specs=[
            pl.BlockSpec((None, 2 * DIFF_HEAD_DIM, ATT_T), lambda b, h, i: (b, h, i)),
            pl.BlockSpec((None, S, 2 * DIFF_HEAD_DIM), lambda b, h, i: (b, 0, h)),
            pl.BlockSpec((None, DIFF_V_DIM, S), lambda b, h, i: (b, h, 0)),
            pl.BlockSpec((None, 2, ATT_T, ATT_T), lambda b, h, i: (h, 0, 0, 0)),
            vec(DIFF_HEAD_DIM), vec(DIFF_HEAD_DIM), vec(DIFF_HEAD_DIM), vec(DIFF_HEAD_DIM),
            vec(DIFF_V_DIM),
        ],
        out_specs=pl.BlockSpec((None, ATT_T, DIFF_V_DIM), lambda b, h, i: (b, i, h)),
        out_shape=jax.ShapeDtypeStruct((B, S, DIFF_WIDTH), _F32),
        scratch_shapes=[
            pltpu.VMEM((1, 2 * ATT_T), _F32),
            pltpu.VMEM((1, 2 * ATT_T), _F32),
            pltpu.VMEM((DIFF_V_DIM, 2 * ATT_T), _F32),
        ],
        compiler_params=pltpu.CompilerParams(
            dimension_semantics=("parallel", "parallel", "arbitrary"),
            vmem_limit_bytes=V7X_VMEM_LIMIT_BYTES),
        name="diffattn",
    )(qt3, k3, vt3, bias_tiles, lq1, lk1, lq2, lk2, subln_g)


def _top16(s, row, payload=None):
    n_rows = s.shape[0]
    groups = range(0, n_rows, SUBLANES)
    vals, rows, pays = [], [], []
    for _ in range(PEER_TOPK):
        v = [s[g:g + SUBLANES] for g in groups]
        extra = [[x[g:g + SUBLANES] for g in groups] for x in ([row] + ([payload] if payload is not None else []))]
        while len(v) > 1:
            keep = [v[a] >= v[a + 1] for a in range(0, len(v) - 1, 2)]
            tail = len(v) % 2
            extra = [[jnp.where(k, x[2 * a], x[2 * a + 1]) for a, k in enumerate(keep)]
                     + ([x[-1]] if tail else []) for x in extra]
            v = ([jnp.maximum(v[2 * a], v[2 * a + 1]) for a in range(len(keep))]
                 + ([v[-1]] if tail else []))
        m = jnp.max(v[0], axis=0, keepdims=True)
        pos = jnp.min(jnp.where(v[0] == m, extra[0][0], n_rows), axis=0, keepdims=True)
        vals.append(m)
        rows.append(pos)
        if payload is not None:
            pays.append(jnp.max(jnp.where(extra[0][0] == pos, extra[1][0], -1), axis=0, keepdims=True))
        s = jnp.where(row == pos, -jnp.inf, s)
    return vals, rows, pays


def _candidates(v1, i1, v2, i2):
    neg = -jnp.inf
    v1c, i1c = jnp.concatenate(v1, axis=0), jnp.concatenate(i1, axis=0)
    v2c, i2c = jnp.concatenate(v2, axis=0), jnp.concatenate(i2, axis=0)
    row = lax.broadcasted_iota(jnp.int32, (SUBLANES, v1c.shape[1]), 0)
    cand = [v1[0] + v2c]
    cidx = [i1[0] * PEER_N_KEYS + i2c]
    for a in range(1, SUBLANES):
        nb = PEER_TOPK // (a + 1)
        cand.append(jnp.where(row < nb, v1[a] + v2c[:SUBLANES], neg))
        cidx.append(i1[a] * PEER_N_KEYS + i2c[:SUBLANES])
    cand.append(v1c[SUBLANES:] + v2[0])
    cidx.append(i1c[SUBLANES:] * PEER_N_KEYS + i2[0])
    return jnp.concatenate(cand, axis=0), jnp.concatenate(cidx, axis=0)


def _route_kernel(x_ref, oa_ref, ob_ref, wo_ref, g_ref, wq_ref, keys_ref, rowid_ref,
                  h1_ref, eid_ref, gate_ref, qp_scr, eid_scr, gate_scr):
    mix = jnp.concatenate([oa_ref[...], ob_ref[...]], axis=-1)
    h1 = x_ref[...] + _mm(mix, wo_ref[...])
    h1_ref[...] = h1
    xn = _rms(h1, g_ref[...])
    qp_scr[...] = _mm(xn, wq_ref[...])
    neg = -jnp.inf

    def head(h, _):
        c0 = pl.multiple_of(h * PEER_DKEY, PEER_DKEY)
        q1 = qp_scr[:, pl.ds(c0, PEER_HALF)]
        q2 = qp_scr[:, pl.ds(c0 + PEER_HALF, PEER_HALF)]
        row = rowid_ref[...]
        v1, i1, _ = _top16(_mm_nt(keys_ref[h, 0], q1), row)
        v2, i2, _ = _top16(_mm_nt(keys_ref[h, 1], q2), row)
        cand, cidx = _candidates(v1, i1, v2, i2)
        best, _, eid = _top16(cand, row[:cand.shape[0]], cidx)
        eid = jnp.concatenate(eid, axis=0)
        best = jnp.concatenate(best, axis=0)
        e = jnp.exp(best - best[0:1])
        gate = e / jnp.sum(e, axis=0, keepdims=True)
        r0 = pl.multiple_of(h * PEER_TOPK, PEER_TOPK)
        eid_scr[pl.ds(r0, PEER_TOPK), :] = eid
        gate_scr[pl.ds(r0, PEER_TOPK), :] = gate
        return 0

    def heads(hg, _):
        for u in range(ROUTE_HEAD_UNROLL):
            head(hg * ROUTE_HEAD_UNROLL + u, 0)
        return 0

    lax.fori_loop(0, PEER_HEADS // ROUTE_HEAD_UNROLL, heads, 0)
    eid_ref[...] = eid_scr[...].T
    gate_ref[...] = gate_scr[...].T


def _route(x2, out_a, out_b, w_out_bf, g_ffn, w_q_bf, keys_bf):
    T = x2.shape[0]
    tm = ROUTE_TM
    full = lambda shape: pl.BlockSpec(shape, lambda i: (0,) * len(shape))
    row_ids = lax.broadcasted_iota(jnp.int32, (PEER_N_KEYS, tm), 0)
    return pl.pallas_call(
        _route_kernel,
        grid=(T // tm,),
        in_specs=[
            pl.BlockSpec((tm, D_MODEL), lambda i: (i, 0)),
            pl.BlockSpec((tm, GMLP_WIDTH), lambda i: (i, 0)),
            pl.BlockSpec((tm, DIFF_WIDTH), lambda i: (i, 0)),
            full((D_MODEL, D_MODEL)),
            full((1, D_MODEL)),
            full((D_MODEL, PEER_HEADS * PEER_DKEY)),
            full((PEER_HEADS, 2, PEER_N_KEYS, PEER_HALF)),
            full((PEER_N_KEYS, tm)),
        ],
        out_specs=[
            pl.BlockSpec((tm, D_MODEL), lambda i: (i, 0)),
            pl.BlockSpec((tm, PEER_SLOTS), lambda i: (i, 0)),
            pl.BlockSpec((tm, PEER_SLOTS), lambda i: (i, 0)),
        ],
        out_shape=[
            jax.ShapeDtypeStruct((T, D_MODEL), _F32),
            jax.ShapeDtypeStruct((T, PEER_SLOTS), jnp.int32),
            jax.ShapeDtypeStruct((T, PEER_SLOTS), _F32),
        ],
        scratch_shapes=[
            pltpu.VMEM((tm, PEER_HEADS * PEER_DKEY), _F32),
            pltpu.VMEM((PEER_SLOTS, tm), jnp.int32),
            pltpu.VMEM((PEER_SLOTS, tm), _F32),
        ],
        compiler_params=pltpu.CompilerParams(
            dimension_semantics=("parallel",), vmem_limit_bytes=V7X_VMEM_LIMIT_BYTES),
        name="route",
    )(x2, out_a, out_b, w_out_bf, g_ffn, w_q_bf, keys_bf, row_ids)


def _peer_kernel(eid_ref, gate_ref, h1_ref, g_ref, ut_ref, v_ref, h2_ref, w_scr, xn_scr):
    j = pl.program_id(1)

    @pl.when(j == 0)
    def _():
        h1 = h1_ref[...]
        xn_scr[...] = _rms(h1, g_ref[...]).astype(_BF)
        h2_ref[...] = h1
        iota = lax.broadcasted_iota(jnp.int32, (PEER_N_KEYS, PEER_SLOTS), 0)
        zeros = jnp.zeros((PEER_N_KEYS, PEER_SLOTS), _BF)

        def onehots(t):
            e_row = eid_ref[pl.ds(t, 1), :]
            pt = jnp.where(iota == (e_row >> 7), gate_ref[pl.ds(t, 1), :], 0.0).astype(_BF)
            qt = jnp.where(iota == (e_row & (PEER_N_KEYS - 1)), 1.0, 0.0).astype(_BF)
            return pt, qt

        nb = PEER_BUILD_TOKENS

        def build(g0, p):
            oh = [onehots((g0 + b) * SUBLANES + p) for b in range(nb)]
            lhs = jnp.concatenate([pt for pt, _ in oh], axis=1)
            rhs_t = jnp.concatenate(
                [jnp.concatenate([oh[b][1] if c == b else zeros for c in range(nb)], axis=1)
                 for b in range(nb)], axis=0)
            w = lax.dot_general(lhs, rhs_t, _NT, preferred_element_type=_F32)
            for b in range(nb):
                w_scr[g0 + b, pl.ds(p, PEER_N_KEYS, stride=SUBLANES), :] = (
                    w[:, b * PEER_N_KEYS:(b + 1) * PEER_N_KEYS])

        def groups(g, _):
            for u in range(PEER_BUILD_UNROLL):
                for p in range(SUBLANES):
                    build((g * PEER_BUILD_UNROLL + u) * nb, p)
            return 0

        lax.fori_loop(0, PEER_TM // (PEER_BUILD_UNROLL * nb * SUBLANES), groups, 0)

    a = jnp.dot(xn_scr[...], ut_ref[...], preferred_element_type=_F32)
    wa = []
    for il in range(PEER_I1_TILE):
        r0 = pl.multiple_of((j * PEER_I1_TILE + il) * SUBLANES, SUBLANES)
        w = w_scr[:, pl.ds(r0, SUBLANES), :].reshape(PEER_TM, PEER_N_KEYS)
        wa.append((w * jax.nn.gelu(a[:, il * PEER_N_KEYS:(il + 1) * PEER_N_KEYS])).astype(_BF))
    h2_ref[...] += jnp.dot(jnp.concatenate(wa, axis=-1), v_ref[...], preferred_element_type=_F32)


def _peer(eid, gate, h1, g_ffn, u_t_bf, v_bf):
    T = h1.shape[0]
    n_exp = v_bf.shape[0]
    te = PEER_I1_TILE * PEER_N_KEYS
    return pl.pallas_call(
        _peer_kernel,
        grid=(T // PEER_TM, n_exp // te),
        in_specs=[
            pl.BlockSpec((PEER_TM, PEER_SLOTS), lambda i, j: (i, 0)),
            pl.BlockSpec((PEER_TM, PEER_SLOTS), lambda i, j: (i, 0)),
            pl.BlockSpec((PEER_TM, D_MODEL), lambda i, j: (i, 0), pipeline_mode=pl.Buffered(1)),
            pl.BlockSpec((1, D_MODEL), lambda i, j: (0, 0)),
            pl.BlockSpec((D_MODEL, te), lambda i, j: (0, j)),
            pl.BlockSpec((te, D_MODEL), lambda i, j: (j, 0)),
        ],
        out_specs=pl.BlockSpec((PEER_TM, D_MODEL), lambda i, j: (i, 0)),
        out_shape=jax.ShapeDtypeStruct((T, D_MODEL), _F32),
        scratch_shapes=[
            pltpu.VMEM((PEER_TM // SUBLANES, PEER_N_KEYS * SUBLANES, PEER_N_KEYS), _F32),
            pltpu.VMEM((PEER_TM, D_MODEL), _BF),
        ],
        compiler_params=pltpu.CompilerParams(
            dimension_semantics=("parallel", "arbitrary"),
            vmem_limit_bytes=V7X_VMEM_LIMIT_BYTES),
        name="peer",
    )(eid, gate, h1, g_ffn, u_t_bf, v_bf)


def _ple_kernel(final, h_ref, p_ref, g_ref, wg_ref, wp_ref, gf_ref, out_ref):
    h = h_ref[...]
    gate = jax.nn.sigmoid(_mm(_rms(h, g_ref[...]), wg_ref[...]))
    h = h + _mm(p_ref[...], wp_ref[...]) * gate
    out_ref[...] = _rms(h, gf_ref[...]) if final else h


def _ple(h2, p2, g_ple, w_gate_bf, w_ple_bf, g_final, final):
    T = h2.shape[0]
    full = lambda shape: pl.BlockSpec(shape, lambda i: (0,) * len(shape))
    return pl.pallas_call(
        functools.partial(_ple_kernel, final),
        grid=(T // PLE_TM,),
        in_specs=[
            pl.BlockSpec((PLE_TM, D_MODEL), lambda i: (i, 0)),
            pl.BlockSpec((PLE_TM, PLE_DIM), lambda i: (i, 0)),
            full((1, D_MODEL)),
            full((D_MODEL, D_MODEL)),
            full((PLE_DIM, D_MODEL)),
            full((1, D_MODEL)),
        ],
        out_specs=pl.BlockSpec((PLE_TM, D_MODEL), lambda i: (i, 0)),
        out_shape=jax.ShapeDtypeStruct((T, D_MODEL), _F32),
        compiler_params=pltpu.CompilerParams(
            dimension_semantics=("parallel",), vmem_limit_bytes=V7X_VMEM_LIMIT_BYTES),
        name="ple",
    )(h2, p2, g_ple, w_gate_bf, w_ple_bf, g_final)


def kernel(x, p, g_mix, w_in, gmlp_ln_g, gmlp_ln_b, gmlp_w_s, gmlp_b_s, gmlp_beta, lambda_q1,
           lambda_k1, lambda_q2, lambda_k2, subln_g, rel_bias, w_out, g_ffn, peer_w_q, peer_keys,
           peer_u, peer_v, g_ple, w_ple, w_gate, g_final):
    B, S, D = x.shape
    depth = w_in.shape[0]
    T = B * S
    row = lambda a: a.reshape(1, -1)
    h = x.reshape(T, D)
    bias_tiles = _relbias(rel_bias)
    for i in range(depth):
        lam_init = 0.8 - 0.6 * math.exp(-0.3 * i)
        out_a, k, qt, vt = _mix_in(h, S, row(g_mix[i]), w_in[i].astype(_BF), row(gmlp_ln_g[i]),
                                row(gmlp_ln_b[i]), gmlp_w_s[i], jnp.transpose(gmlp_b_s[i]),
                                row(gmlp_beta[i]))
        out_b = _diffattn(k.reshape(B, S, Q_COLS), qt, vt, bias_tiles, row(lambda_q1[i]),
                          row(lambda_k1[i]), row(lambda_q2[i]), row(lambda_k2[i]),
                          row(subln_g[i]), lam_init)
        h1, eid, gate = _route(h, out_a, out_b.reshape(T, DIFF_WIDTH), w_out[i].astype(_BF),
                               row(g_ffn[i]), peer_w_q[i].astype(_BF), peer_keys[i].astype(_BF))
        h2 = _peer(eid, gate, h1, row(g_ffn[i]), jnp.transpose(peer_u[i]).astype(_BF),
                   peer_v[i].astype(_BF))
        h = _ple(h2, p[i].reshape(T, PLE_DIM), row(g_ple[i]), w_gate[i].astype(_BF),
                 w_ple[i].astype(_BF), row(g_final), i == depth - 1)
    return h.reshape(B, S, D)
```

```python
import functools
import math

import jax
import jax.numpy as jnp
from jax import lax
from jax.experimental import pallas as pl
from jax.experimental.pallas import tpu as pltpu

D_MODEL = 1024
CHUNK = 64
GMLP_WIDTH = 512
GMLP_GROUPS = 4
GMLP_GROUP_CH = 128
GMLP_BLOCK = 128
DIFF_HEADS = 4
DIFF_HEAD_DIM = 64
DIFF_V_DIM = 128
DIFF_WIDTH = 512
A_COLS = 1024
Q_COLS = 512
V_COLS = 512
IN_COLS = A_COLS + 2 * Q_COLS + V_COLS
REL_BUCKETS = 32
REL_MAX_EXACT = 8
PEER_N_KEYS = 128
PEER_HEADS = 8
PEER_TOPK = 16
PEER_HALF = 128
PEER_DKEY = 256
PEER_SLOTS = PEER_HEADS * PEER_TOPK
PLE_DIM = 256
EPS = 1e-6
NEG_INF = -1e30
SUBLANES = 8

V7X_VMEM_LIMIT_BYTES = 56 * 1024 * 1024

MIX_TM = 256
ATT_T = 512
ATT_HEADS = 4
RELBIAS_ROWS = 128
ROUTE_TM = 256
ROUTE_HEAD_UNROLL = 4
PEER_TM = 512
PEER_I1_TILE = 8
PEER_BUILD_TOKENS = 2
PEER_BUILD_UNROLL = 4
PLE_TM = 512

_NT = (((1,), (1,)), ((), ()))
_BF = jnp.bfloat16
_F32 = jnp.float32


def _rms(x, g):
    return x * lax.rsqrt(jnp.mean(x * x, axis=-1, keepdims=True) + EPS) * g


def _mm(a, b):
    return jnp.dot(a.astype(_BF), b.astype(_BF), preferred_element_type=_F32)


def _mm_nt(a, b):
    return lax.dot_general(a.astype(_BF), b.astype(_BF), _NT, preferred_element_type=_F32)


def _mix_in_kernel(x_ref, g_ref, w_ref, lng_ref, lnb_ref, ws_ref, bst_ref, beta_ref,
                   outa_ref, k_ref, qt_ref, vt_ref):
    n1 = _rms(x_ref[...], g_ref[...])
    z = jnp.dot(n1.astype(_BF), w_ref[...], preferred_element_type=_F32)
    qt_ref[...] = z[:, A_COLS:A_COLS + Q_COLS].T.astype(_BF)
    k_ref[...] = z[:, A_COLS + Q_COLS:A_COLS + 2 * Q_COLS].astype(_BF)
    vt_ref[...] = z[:, A_COLS + 2 * Q_COLS:].T.astype(_BF)
    za = jax.nn.gelu(z[:, :A_COLS])
    pos_i = lax.broadcasted_iota(jnp.int32, (GMLP_BLOCK, GMLP_BLOCK), 0) // CHUNK
    pos_j = lax.broadcasted_iota(jnp.int32, (GMLP_BLOCK, GMLP_BLOCK), 1) // CHUNK
    causal = pos_j <= pos_i
    for g in range(GMLP_GROUPS):
        c0 = g * GMLP_GROUP_CH
        u = za[:, c0:c0 + GMLP_GROUP_CH]
        v = za[:, GMLP_WIDTH + c0:GMLP_WIDTH + c0 + GMLP_GROUP_CH]
        mu = jnp.mean(v, axis=-1, keepdims=True)
        vc = v - mu
        vn = vc * lax.rsqrt(jnp.mean(vc * vc, axis=-1, keepdims=True) + EPS)
        vn = vn * lng_ref[:, c0:c0 + GMLP_GROUP_CH] + lnb_ref[:, c0:c0 + GMLP_GROUP_CH]
        wm = jnp.where(causal, ws_ref[g], 0.0)
        for n in range(MIX_TM // GMLP_BLOCK):
            r0 = n * GMLP_BLOCK
            sv = _mm(wm, vn[r0:r0 + GMLP_BLOCK]) + bst_ref[:, g:g + 1]
            o = u[r0:r0 + GMLP_BLOCK] * sv
            outa_ref[r0:r0 + GMLP_BLOCK, c0:c0 + GMLP_GROUP_CH] = _rms(
                o, beta_ref[:, c0:c0 + GMLP_GROUP_CH])


def _mix_in(x2, seq, g_mix, w_in_bf, ln_g, ln_b, w_s, b_s_t, beta):
    T = x2.shape[0]
    n_seq = seq // MIX_TM
    full = lambda shape: pl.BlockSpec(shape, lambda i: (0,) * len(shape))
    return pl.pallas_call(
        _mix_in_kernel,
        grid=(T // MIX_TM,),
        in_specs=[
            pl.BlockSpec((MIX_TM, D_MODEL), lambda i: (i, 0)),
            full((1, D_MODEL)),
            full((D_MODEL, IN_COLS)),
            full((1, GMLP_WIDTH)),
            full((1, GMLP_WIDTH)),
            full((GMLP_GROUPS, GMLP_BLOCK, GMLP_BLOCK)),
            full((GMLP_BLOCK, GMLP_GROUPS)),
            full((1, GMLP_WIDTH)),
        ],
        out_specs=[
            pl.BlockSpec((MIX_TM, GMLP_WIDTH), lambda i: (i, 0)),
            pl.BlockSpec((MIX_TM, Q_COLS), lambda i: (i, 0)),
            pl.BlockSpec((None, Q_COLS, MIX_TM), lambda i: (i // n_seq, 0, i % n_seq)),
            pl.BlockSpec((None, V_COLS, MIX_TM), lambda i: (i // n_seq, 0, i % n_seq)),
        ],
        out_shape=[
            jax.ShapeDtypeStruct((T, GMLP_WIDTH), _F32),
            jax.ShapeDtypeStruct((T, Q_COLS), _BF),
            jax.ShapeDtypeStruct((T // seq, Q_COLS, seq), _BF),
            jax.ShapeDtypeStruct((T // seq, V_COLS, seq), _BF),
        ],
        compiler_params=pltpu.CompilerParams(
            dimension_semantics=("parallel",), vmem_limit_bytes=V7X_VMEM_LIMIT_BYTES),
        name="mix_in",
    )(x2, g_mix, w_in_bf, ln_g, ln_b, w_s, b_s_t, beta)


def _relbias_kernel(rb_ref, out_ref):
    h, d, r = pl.program_id(0), pl.program_id(1), pl.program_id(2)
    kj = lax.broadcasted_iota(jnp.int32, (RELBIAS_ROWS, ATT_T), 0) + r * RELBIAS_ROWS
    qi = lax.broadcasted_iota(jnp.int32, (RELBIAS_ROWS, ATT_T), 1)
    rel = kj - qi - d * ATT_T
    n = jnp.abs(rel)
    n2 = n * n
    large = jnp.full_like(n, REL_MAX_EXACT)
    for k in range(1, 8):
        large = large + (n2 >= (1 << (6 + k))).astype(jnp.int32)
    bucket = jnp.where(rel > 0, REL_BUCKETS // 2, 0) + jnp.where(n < REL_MAX_EXACT, n, large)
    bias = jnp.zeros(rel.shape, _F32)
    for b in range(REL_BUCKETS):
        bias = jnp.where(bucket == b, rb_ref[b, h], bias)
    bias = bias - rb_ref[REL_BUCKETS // 2 - 1, h]
    visible = ((kj - d * ATT_T) // CHUNK) <= (qi // CHUNK)
    out_ref[...] = jnp.where(visible, bias, NEG_INF)


def _relbias(rel_bias):
    return pl.pallas_call(
        _relbias_kernel,
        grid=(DIFF_HEADS, 2, ATT_T // RELBIAS_ROWS),
        in_specs=[pl.BlockSpec(memory_space=pltpu.SMEM)],
        out_specs=pl.BlockSpec((None, None, RELBIAS_ROWS, ATT_T), lambda h, d, r: (h, d, r, 0)),
        out_shape=jax.ShapeDtypeStruct((DIFF_HEADS, 2, ATT_T, ATT_T), _F32),
        name="relbias",
    )(rel_bias)


def _diffattn_kernel(lam_init, qt_ref, k_ref, vt_ref, bias_ref, lq1_ref, lk1_ref, lq2_ref, lk2_ref,
                     sg_ref, out_ref, m_scr, l_scr, acc_scr):
    qi = pl.program_id(2)
    hd = 2 * DIFF_HEAD_DIM
    q2x = []
    for hh in range(ATT_HEADS):
        qt = qt_ref[hh * hd:(hh + 1) * hd, :] * (DIFF_HEAD_DIM ** -0.5)
        chan = lax.broadcasted_iota(jnp.int32, qt.shape, 0)
        zero = jnp.zeros_like(qt)
        q2x.append(jnp.concatenate([jnp.where(chan < DIFF_HEAD_DIM, qt, zero),
                                    jnp.where(chan >= DIFF_HEAD_DIM, qt, zero)], axis=1))

    def step(j, d, first):
        r0 = pl.multiple_of(j * ATT_T, ATT_T)
        logits = []
        for hh in range(ATT_HEADS):
            kt = k_ref[pl.ds(r0, ATT_T), hh * hd:(hh + 1) * hd]
            s = jnp.dot(kt, q2x[hh], preferred_element_type=_F32)
            if d is not None:
                bias = bias_ref[hh, d]
                s = s + jnp.concatenate([bias, bias], axis=1)
            logits.append(s)
        for hh in range(ATT_HEADS):
            s = logits[hh]
            vt = vt_ref[hh * DIFF_V_DIM:(hh + 1) * DIFF_V_DIM, pl.ds(r0, ATT_T)]
            s_max = jnp.max(s, axis=0, keepdims=True)
            if first:
                m_new = s_max
                p = jnp.exp(s - m_new)
                l_scr[hh] = jnp.sum(p, axis=0, keepdims=True)
                acc_scr[hh] = jnp.dot(vt, p.astype(_BF), preferred_element_type=_F32)
            else:
                m_old = m_scr[hh]
                m_new = jnp.maximum(m_old, s_max)
                a = jnp.exp(m_old - m_new)
                p = jnp.exp(s - m_new)
                l_scr[hh] = a * l_scr[hh] + jnp.sum(p, axis=0, keepdims=True)
                acc_scr[hh] = a * acc_scr[hh] + jnp.dot(vt, p.astype(_BF),
                                                        preferred_element_type=_F32)
            m_scr[hh] = m_new

    step(qi, 0, True)

    @pl.when(qi > 0)
    def _():
        step(qi - 1, 1, False)

    def far(j, _):
        step(j, None, False)
        return 0

    lax.fori_loop(0, qi - 1, far, 0)

    lam = (jnp.exp(jnp.sum(lq1_ref[...] * lk1_ref[...], axis=-1, keepdims=True))
           - jnp.exp(jnp.sum(lq2_ref[...] * lk2_ref[...], axis=-1, keepdims=True)) + lam_init)
    for hh in range(ATT_HEADS):
        o = acc_scr[hh] / l_scr[hh]
        o = o[:, :ATT_T] - lam * o[:, ATT_T:]
        o = o * lax.rsqrt(jnp.mean(o * o, axis=0, keepdims=True) + EPS)
        out_ref[:, hh * DIFF_V_DIM:(hh + 1) * DIFF_V_DIM] = o.T * sg_ref[...] * (1.0 - lam_init)


def _diffattn(k3, qt3, vt3, bias_tiles, lq1, lk1, lq2, lk2, subln_g, lam_init):
    B, S, _ = k3.shape
    nq = S // ATT_T
    vec = lambda n: pl.BlockSpec((1, n), lambda b, h, i: (0, 0))
    return pl.pallas_call(
        functools.partial(_diffattn_kernel, lam_init),
        grid=(B, DIFF_HEADS // ATT_HEADS, nq),
        in_specs=[
            pl.BlockSpec((None, ATT_HEADS * 2 * DIFF_HEAD_DIM, ATT_T), lambda b, h, i: (b, h, i)),
            pl.BlockSpec((None, S, ATT_HEADS * 2 * DIFF_HEAD_DIM), lambda b, h, i: (b, 0, h)),
            pl.BlockSpec((None, ATT_HEADS * DIFF_V_DIM, S), lambda b, h, i: (b, h, 0)),
            pl.BlockSpec((ATT_HEADS, 2, ATT_T, ATT_T), lambda b, h, i: (h, 0, 0, 0)),
            vec(DIFF_HEAD_DIM), vec(DIFF_HEAD_DIM), vec(DIFF_HEAD_DIM), vec(DIFF_HEAD_DIM),
            vec(DIFF_V_DIM),
        ],
        out_specs=pl.BlockSpec((None, ATT_T, ATT_HEADS * DIFF_V_DIM), lambda b, h, i: (b, i, h)),
        out_shape=jax.ShapeDtypeStruct((B, S, DIFF_WIDTH), _F32),
        scratch_shapes=[
            pltpu.VMEM((ATT_HEADS, 1, 2 * ATT_T), _F32),
            pltpu.VMEM((ATT_HEADS, 1, 2 * ATT_T), _F32),
            pltpu.VMEM((ATT_HEADS, DIFF_V_DIM, 2 * ATT_T), _F32),
        ],
        compiler_params=pltpu.CompilerParams(
            dimension_semantics=("parallel", "parallel", "arbitrary"),
            vmem_limit_bytes=V7X_VMEM_LIMIT_BYTES),
        name="diffattn",
    )(qt3, k3, vt3, bias_tiles, lq1, lk1, lq2, lk2, subln_g)


def _top16(s, row, payload=None):
    n_rows = s.shape[0]
    groups = range(0, n_rows, SUBLANES)
    vals, rows, pays = [], [], []
    for _ in range(PEER_TOPK):
        v = [s[g:g + SUBLANES] for g in groups]
        extra = [[x[g:g + SUBLANES] for g in groups] for x in ([row] + ([payload] if payload is not None else []))]
        while len(v) > 1:
            keep = [v[a] >= v[a + 1] for a in range(0, len(v) - 1, 2)]
            tail = len(v) % 2
            extra = [[jnp.where(k, x[2 * a], x[2 * a + 1]) for a, k in enumerate(keep)]
                     + ([x[-1]] if tail else []) for x in extra]
            v = ([jnp.maximum(v[2 * a], v[2 * a + 1]) for a in range(len(keep))]
                 + ([v[-1]] if tail else []))
        m = jnp.max(v[0], axis=0, keepdims=True)
        pos = jnp.min(jnp.where(v[0] == m, extra[0][0], n_rows), axis=0, keepdims=True)
        vals.append(m)
        rows.append(pos)
        if payload is not None:
            pays.append(jnp.max(jnp.where(extra[0][0] == pos, extra[1][0], -1), axis=0, keepdims=True))
        s = jnp.where(row == pos, -jnp.inf, s)
    return vals, rows, pays


def _candidates(v1, i1, v2, i2):
    neg = -jnp.inf
    v1c, i1c = jnp.concatenate(v1, axis=0), jnp.concatenate(i1, axis=0)
    v2c, i2c = jnp.concatenate(v2, axis=0), jnp.concatenate(i2, axis=0)
    row = lax.broadcasted_iota(jnp.int32, (SUBLANES, v1c.shape[1]), 0)
    cand = [v1[0] + v2c]
    cidx = [i1[0] * PEER_N_KEYS + i2c]
    for a in range(1, SUBLANES):
        nb = PEER_TOPK // (a + 1)
        cand.append(jnp.where(row < nb, v1[a] + v2c[:SUBLANES], neg))
        cidx.append(i1[a] * PEER_N_KEYS + i2c[:SUBLANES])
    cand.append(v1c[SUBLANES:] + v2[0])
    cidx.append(i1c[SUBLANES:] * PEER_N_KEYS + i2[0])
    return jnp.concatenate(cand, axis=0), jnp.concatenate(cidx, axis=0)


def _route_kernel(x_ref, oa_ref, ob_ref, wo_ref, g_ref, wq_ref, keys_ref, rowid_ref,
                  h1_ref, eid_ref, gate_ref, qp_scr, eid_scr, gate_scr):
    mix = jnp.concatenate([oa_ref[...], ob_ref[...]], axis=-1)
    h1 = x_ref[...] + _mm(mix, wo_ref[...])
    h1_ref[...] = h1
    xn = _rms(h1, g_ref[...])
    qp_scr[...] = _mm(xn, wq_ref[...])
    neg = -jnp.inf

    def head(h, _):
        c0 = pl.multiple_of(h * PEER_DKEY, PEER_DKEY)
        q1 = qp_scr[:, pl.ds(c0, PEER_HALF)]
        q2 = qp_scr[:, pl.ds(c0 + PEER_HALF, PEER_HALF)]
        row = rowid_ref[...]
        v1, i1, _ = _top16(_mm_nt(keys_ref[h, 0], q1), row)
        v2, i2, _ = _top16(_mm_nt(keys_ref[h, 1], q2), row)
        cand, cidx = _candidates(v1, i1, v2, i2)
        best, _, eid = _top16(cand, row[:cand.shape[0]], cidx)
        eid = jnp.concatenate(eid, axis=0)
        best = jnp.concatenate(best, axis=0)
        e = jnp.exp(best - best[0:1])
        gate = e / jnp.sum(e, axis=0, keepdims=True)
        r0 = pl.multiple_of(h * PEER_TOPK, PEER_TOPK)
        eid_scr[pl.ds(r0, PEER_TOPK), :] = eid
        gate_scr[pl.ds(r0, PEER_TOPK), :] = gate
        return 0

    def heads(hg, _):
        for u in range(ROUTE_HEAD_UNROLL):
            head(hg * ROUTE_HEAD_UNROLL + u, 0)
        return 0

    lax.fori_loop(0, PEER_HEADS // ROUTE_HEAD_UNROLL, heads, 0)
    eid_ref[...] = eid_scr[...].T
    gate_ref[...] = gate_scr[...].T


def _route(x2, out_a, out_b, w_out_bf, g_ffn, w_q_bf, keys_bf):
    T = x2.shape[0]
    tm = ROUTE_TM
    full = lambda shape: pl.BlockSpec(shape, lambda i: (0,) * len(shape))
    row_ids = lax.broadcasted_iota(jnp.int32, (PEER_N_KEYS, tm), 0)
    return pl.pallas_call(
        _route_kernel,
        grid=(T // tm,),
        in_specs=[
            pl.BlockSpec((tm, D_MODEL), lambda i: (i, 0)),
            pl.BlockSpec((tm, GMLP_WIDTH), lambda i: (i, 0)),
            pl.BlockSpec((tm, DIFF_WIDTH), lambda i: (i, 0)),
            full((D_MODEL, D_MODEL)),
            full((1, D_MODEL)),
            full((D_MODEL, PEER_HEADS * PEER_DKEY)),
            full((PEER_HEADS, 2, PEER_N_KEYS, PEER_HALF)),
            full((PEER_N_KEYS, tm)),
        ],
        out_specs=[
            pl.BlockSpec((tm, D_MODEL), lambda i: (i, 0)),
            pl.BlockSpec((tm, PEER_SLOTS), lambda i: (i, 0)),
            pl.BlockSpec((tm, PEER_SLOTS), lambda i: (i, 0)),
        ],
        out_shape=[
            jax.ShapeDtypeStruct((T, D_MODEL), _F32),
            jax.ShapeDtypeStruct((T, PEER_SLOTS), jnp.int32),
            jax.ShapeDtypeStruct((T, PEER_SLOTS), _F32),
        ],
        scratch_shapes=[
            pltpu.VMEM((tm, PEER_HEADS * PEER_DKEY), _F32),
            pltpu.VMEM((PEER_SLOTS, tm), jnp.int32),
            pltpu.VMEM((PEER_SLOTS, tm), _F32),
        ],
        compiler_params=pltpu.CompilerParams(
            dimension_semantics=("parallel",), vmem_limit_bytes=V7X_VMEM_LIMIT_BYTES),
        name="route",
    )(x2, out_a, out_b, w_out_bf, g_ffn, w_q_bf, keys_bf, row_ids)


def _peer_kernel(eid_ref, gate_ref, h1_ref, g_ref, ut_ref, v_ref, h2_ref, w_scr, xn_scr):
    j = pl.program_id(1)

    @pl.when(j == 0)
    def _():
        h1 = h1_ref[...]
        xn_scr[...] = _rms(h1, g_ref[...]).astype(_BF)
        h2_ref[...] = h1
        iota = lax.broadcasted_iota(jnp.int32, (PEER_N_KEYS, PEER_SLOTS), 0)
        zeros = jnp.zeros((PEER_N_KEYS, PEER_SLOTS), _BF)

        def onehots(t):
            e_row = eid_ref[pl.ds(t, 1), :]
            pt = jnp.where(iota == (e_row >> 7), gate_ref[pl.ds(t, 1), :], 0.0).astype(_BF)
            qt = jnp.where(iota == (e_row & (PEER_N_KEYS - 1)), 1.0, 0.0).astype(_BF)
            return pt, qt

        nb = PEER_BUILD_TOKENS

        def build(g0, p):
            oh = [onehots((g0 + b) * SUBLANES + p) for b in range(nb)]
            lhs = jnp.concatenate([pt for pt, _ in oh], axis=1)
            rhs_t = jnp.concatenate(
                [jnp.concatenate([oh[b][1] if c == b else zeros for c in range(nb)], axis=1)
                 for b in range(nb)], axis=0)
            w = lax.dot_general(lhs, rhs_t, _NT, preferred_element_type=_F32)
            for b in range(nb):
                w_scr[g0 + b, pl.ds(p, PEER_N_KEYS, stride=SUBLANES), :] = (
                    w[:, b * PEER_N_KEYS:(b + 1) * PEER_N_KEYS])

        def groups(g, _):
            for u in range(PEER_BUILD_UNROLL):
                for p in range(SUBLANES):
                    build((g * PEER_BUILD_UNROLL + u) * nb, p)
            return 0

        lax.fori_loop(0, PEER_TM // (PEER_BUILD_UNROLL * nb * SUBLANES), groups, 0)

    a = jnp.dot(xn_scr[...], ut_ref[...], preferred_element_type=_F32)
    wa = []
    for il in range(PEER_I1_TILE):
        r0 = pl.multiple_of((j * PEER_I1_TILE + il) * SUBLANES, SUBLANES)
        w = w_scr[:, pl.ds(r0, SUBLANES), :].reshape(PEER_TM, PEER_N_KEYS)
        wa.append((w * jax.nn.gelu(a[:, il * PEER_N_KEYS:(il + 1) * PEER_N_KEYS])).astype(_BF))
    h2_ref[...] += jnp.dot(jnp.concatenate(wa, axis=-1), v_ref[...], preferred_element_type=_F32)


def _peer(eid, gate, h1, g_ffn, u_t_bf, v_bf):
    T = h1.shape[0]
    n_exp = v_bf.shape[0]
    te = PEER_I1_TILE * PEER_N_KEYS
    return pl.pallas_call(
        _peer_kernel,
        grid=(T // PEER_TM, n_exp // te),
        in_specs=[
            pl.BlockSpec((PEER_TM, PEER_SLOTS), lambda i, j: (i, 0)),
            pl.BlockSpec((PEER_TM, PEER_SLOTS), lambda i, j: (i, 0)),
            pl.BlockSpec((PEER_TM, D_MODEL), lambda i, j: (i, 0), pipeline_mode=pl.Buffered(1)),
            pl.BlockSpec((1, D_MODEL), lambda i, j: (0, 0)),
            pl.BlockSpec((D_MODEL, te), lambda i, j: (0, j)),
            pl.BlockSpec((te, D_MODEL), lambda i, j: (j, 0)),
        ],
        out_specs=pl.BlockSpec((PEER_TM, D_MODEL), lambda i, j: (i, 0)),
        out_shape=jax.ShapeDtypeStruct((T, D_MODEL), _F32),
        scratch_shapes=[
            pltpu.VMEM((PEER_TM // SUBLANES, PEER_N_KEYS * SUBLANES, PEER_N_KEYS), _F32),
            pltpu.VMEM((PEER_TM, D_MODEL), _BF),
        ],
        compiler_params=pltpu.CompilerParams(
            dimension_semantics=("parallel", "arbitrary"),
            vmem_limit_bytes=V7X_VMEM_LIMIT_BYTES),
        name="peer",
    )(eid, gate, h1, g_ffn, u_t_bf, v_bf)


def _ple_kernel(final, h_ref, p_ref, g_ref, wg_ref, wp_ref, gf_ref, out_ref):
    h = h_ref[...]
    gate = jax.nn.sigmoid(_mm(_rms(h, g_ref[...]), wg_ref[...]))
    h = h + _mm(p_ref[...], wp_ref[...]) * gate
    out_ref[...] = _rms(h, gf_ref[...]) if final else h


def _ple(h2, p2, g_ple, w_gate_bf, w_ple_bf, g_final, final):
    T = h2.shape[0]
    full = lambda shape: pl.BlockSpec(shape, lambda i: (0,) * len(shape))
    return pl.pallas_call(
        functools.partial(_ple_kernel, final),
        grid=(T // PLE_TM,),
        in_specs=[
            pl.BlockSpec((PLE_TM, D_MODEL), lambda i: (i, 0)),
            pl.BlockSpec((PLE_TM, PLE_DIM), lambda i: (i, 0)),
            full((1, D_MODEL)),
            full((D_MODEL, D_MODEL)),
            full((PLE_DIM, D_MODEL)),
            full((1, D_MODEL)),
        ],
        out_specs=pl.BlockSpec((PLE_TM, D_MODEL), lambda i: (i, 0)),
        out_shape=jax.ShapeDtypeStruct((T, D_MODEL), _F32),
        compiler_params=pltpu.CompilerParams(
            dimension_semantics=("parallel",), vmem_limit_bytes=V7X_VMEM_LIMIT_BYTES),
        name="ple",
    )(h2, p2, g_ple, w_gate_bf, w_ple_bf, g_final)


def kernel(x, p, g_mix, w_in, gmlp_ln_g, gmlp_ln_b, gmlp_w_s, gmlp_b_s, gmlp_beta, lambda_q1,
           lambda_k1, lambda_q2, lambda_k2, subln_g, rel_bias, w_out, g_ffn, peer_w_q, peer_keys,
           peer_u, peer_v, g_ple, w_ple, w_gate, g_final):
    B, S, D = x.shape
    depth = w_in.shape[0]
    T = B * S
    row = lambda a: a.reshape(1, -1)
    h = x.reshape(T, D)
    bias_tiles = _relbias(rel_bias)
    for i in range(depth):
        lam_init = 0.8 - 0.6 * math.exp(-0.3 * i)
        out_a, k, qt, vt = _mix_in(h, S, row(g_mix[i]), w_in[i].astype(_BF), row(gmlp_ln_g[i]),
                                row(gmlp_ln_b[i]), gmlp_w_s[i], jnp.transpose(gmlp_b_s[i]),
                                row(gmlp_beta[i]))
        out_b = _diffattn(k.reshape(B, S, Q_COLS), qt, vt, bias_tiles, row(lambda_q1[i]),
                          row(lambda_k1[i]), row(lambda_q2[i]), row(lambda_k2[i]),
                          row(subln_g[i]), lam_init)
        h1, eid, gate = _route(h, out_a, out_b.reshape(T, DIFF_WIDTH), w_out[i].astype(_BF),
                               row(g_ffn[i]), peer_w_q[i].astype(_BF), peer_keys[i].astype(_BF))
        h2 = _peer(eid, gate, h1, row(g_ffn[i]), jnp.transpose(peer_u[i].astype(_BF)),
                   peer_v[i].astype(_BF))
        h = _ple(h2, p[i].reshape(T, PLE_DIM), row(g_ple[i]), w_gate[i].astype(_BF),
                 w_ple[i].astype(_BF), row(g_final), i == depth - 1)
    return h.reshape(B, S, D)
```

```python
import functools
import math

import jax
import jax.numpy as jnp
from jax import lax
from jax.experimental import pallas as pl
from jax.experimental.pallas import tpu as pltpu

D_MODEL = 1024
CHUNK = 64
GMLP_WIDTH = 512
GMLP_GROUPS = 4
GMLP_GROUP_CH = 128
GMLP_BLOCK = 128
DIFF_HEADS = 4
DIFF_HEAD_DIM = 64
DIFF_V_DIM = 128
DIFF_WIDTH = 512
A_COLS = 1024
Q_COLS = 512
V_COLS = 512
IN_COLS = A_COLS + 2 * Q_COLS + V_COLS
REL_BUCKETS = 32
REL_MAX_EXACT = 8
PEER_N_KEYS = 128
PEER_HEADS = 8
PEER_TOPK = 16
PEER_HALF = 128
PEER_DKEY = 256
PEER_SLOTS = PEER_HEADS * PEER_TOPK
PLE_DIM = 256
EPS = 1e-6
NEG_INF = -1e30
SUBLANES = 8

V7X_VMEM_LIMIT_BYTES = 56 * 1024 * 1024

MIX_TM = 512
ATT_T = 512
ATT_HEADS = 4
RELBIAS_ROWS = 128
ROUTE_TM = 256
ROUTE_HEAD_UNROLL = 4
PEER_TM = 512
PEER_I1_TILE = 8
PEER_BUILD_TOKENS = 2
PEER_BUILD_UNROLL = 4

_NT = (((1,), (1,)), ((), ()))
_BF = jnp.bfloat16
_F32 = jnp.float32


def _rms(x, g):
    return x * lax.rsqrt(jnp.mean(x * x, axis=-1, keepdims=True) + EPS) * g


def _mm(a, b):
    return jnp.dot(a.astype(_BF), b.astype(_BF), preferred_element_type=_F32)


def _mm_nt(a, b):
    return lax.dot_general(a.astype(_BF), b.astype(_BF), _NT, preferred_element_type=_F32)


def _mix_in_kernel(x_ref, g_ref, w_ref, lng_ref, lnb_ref, ws_ref, bst_ref, beta_ref,
                   outa_ref, k_ref, qt_ref, vt_ref):
    n1 = _rms(x_ref[...], g_ref[...])
    z = jnp.dot(n1.astype(_BF), w_ref[...], preferred_element_type=_F32)
    qt_ref[...] = z[:, A_COLS:A_COLS + Q_COLS].T.astype(_BF)
    k_ref[...] = z[:, A_COLS + Q_COLS:A_COLS + 2 * Q_COLS].astype(_BF)
    vt_ref[...] = z[:, A_COLS + 2 * Q_COLS:].T.astype(_BF)
    za = jax.nn.gelu(z[:, :A_COLS])
    pos_i = lax.broadcasted_iota(jnp.int32, (GMLP_BLOCK, GMLP_BLOCK), 0) // CHUNK
    pos_j = lax.broadcasted_iota(jnp.int32, (GMLP_BLOCK, GMLP_BLOCK), 1) // CHUNK
    causal = pos_j <= pos_i
    for g in range(GMLP_GROUPS):
        c0 = g * GMLP_GROUP_CH
        u = za[:, c0:c0 + GMLP_GROUP_CH]
        v = za[:, GMLP_WIDTH + c0:GMLP_WIDTH + c0 + GMLP_GROUP_CH]
        mu = jnp.mean(v, axis=-1, keepdims=True)
        vc = v - mu
        vn = vc * lax.rsqrt(jnp.mean(vc * vc, axis=-1, keepdims=True) + EPS)
        vn = vn * lng_ref[:, c0:c0 + GMLP_GROUP_CH] + lnb_ref[:, c0:c0 + GMLP_GROUP_CH]
        wm = jnp.where(causal, ws_ref[g], 0.0)
        for n in range(MIX_TM // GMLP_BLOCK):
            r0 = n * GMLP_BLOCK
            sv = _mm(wm, vn[r0:r0 + GMLP_BLOCK]) + bst_ref[:, g:g + 1]
            o = u[r0:r0 + GMLP_BLOCK] * sv
            outa_ref[r0:r0 + GMLP_BLOCK, c0:c0 + GMLP_GROUP_CH] = _rms(
                o, beta_ref[:, c0:c0 + GMLP_GROUP_CH])


def _mix_in(x2, seq, g_mix, w_in_bf, ln_g, ln_b, w_s, b_s_t, beta):
    T = x2.shape[0]
    n_seq = seq // MIX_TM
    full = lambda shape: pl.BlockSpec(shape, lambda i: (0,) * len(shape))
    return pl.pallas_call(
        _mix_in_kernel,
        grid=(T // MIX_TM,),
        in_specs=[
            pl.BlockSpec((MIX_TM, D_MODEL), lambda i: (i, 0)),
            full((1, D_MODEL)),
            full((D_MODEL, IN_COLS)),
            full((1, GMLP_WIDTH)),
            full((1, GMLP_WIDTH)),
            full((GMLP_GROUPS, GMLP_BLOCK, GMLP_BLOCK)),
            full((GMLP_BLOCK, GMLP_GROUPS)),
            full((1, GMLP_WIDTH)),
        ],
        out_specs=[
            pl.BlockSpec((MIX_TM, GMLP_WIDTH), lambda i: (i, 0)),
            pl.BlockSpec((MIX_TM, Q_COLS), lambda i: (i, 0)),
            pl.BlockSpec((None, Q_COLS, MIX_TM), lambda i: (i // n_seq, 0, i % n_seq)),
            pl.BlockSpec((None, V_COLS, MIX_TM), lambda i: (i // n_seq, 0, i % n_seq)),
        ],
        out_shape=[
            jax.ShapeDtypeStruct((T, GMLP_WIDTH), _F32),
            jax.ShapeDtypeStruct((T, Q_COLS), _BF),
            jax.ShapeDtypeStruct((T // seq, Q_COLS, seq), _BF),
            jax.ShapeDtypeStruct((T // seq, V_COLS, seq), _BF),
        ],
        compiler_params=pltpu.CompilerParams(
            dimension_semantics=("parallel",), vmem_limit_bytes=V7X_VMEM_LIMIT_BYTES),
        name="mix_in",
    )(x2, g_mix, w_in_bf, ln_g, ln_b, w_s, b_s_t, beta)


def _relbias_kernel(rb_ref, out_ref):
    h, d, r = pl.program_id(0), pl.program_id(1), pl.program_id(2)
    kj = lax.broadcasted_iota(jnp.int32, (RELBIAS_ROWS, ATT_T), 0) + r * RELBIAS_ROWS
    qi = lax.broadcasted_iota(jnp.int32, (RELBIAS_ROWS, ATT_T), 1)
    rel = kj - qi - d * ATT_T
    n = jnp.abs(rel)
    n2 = n * n
    large = jnp.full_like(n, REL_MAX_EXACT)
    for k in range(1, 8):
        large = large + (n2 >= (1 << (6 + k))).astype(jnp.int32)
    bucket = jnp.where(rel > 0, REL_BUCKETS // 2, 0) + jnp.where(n < REL_MAX_EXACT, n, large)
    bias = jnp.zeros(rel.shape, _F32)
    for b in range(REL_BUCKETS):
        bias = jnp.where(bucket == b, rb_ref[b, h], bias)
    bias = bias - rb_ref[REL_BUCKETS // 2 - 1, h]
    visible = ((kj - d * ATT_T) // CHUNK) <= (qi // CHUNK)
    out_ref[...] = jnp.where(visible, bias, NEG_INF)


def _relbias(rel_bias):
    return pl.pallas_call(
        _relbias_kernel,
        grid=(DIFF_HEADS, 2, ATT_T // RELBIAS_ROWS),
        in_specs=[pl.BlockSpec(memory_space=pltpu.SMEM)],
        out_specs=pl.BlockSpec((None, None, RELBIAS_ROWS, ATT_T), lambda h, d, r: (h, d, r, 0)),
        out_shape=jax.ShapeDtypeStruct((DIFF_HEADS, 2, ATT_T, ATT_T), _F32),
        name="relbias",
    )(rel_bias)


def _diffattn_kernel(lam_init, qt_ref, k_ref, vt_ref, bias_ref, lq1_ref, lk1_ref, lq2_ref, lk2_ref,
                     sg_ref, out_ref, m_scr, l_scr, acc_scr):
    qi = pl.program_id(2)
    hd = 2 * DIFF_HEAD_DIM
    q2x = []
    for hh in range(ATT_HEADS):
        qt = qt_ref[hh * hd:(hh + 1) * hd, :] * (DIFF_HEAD_DIM ** -0.5)
        chan = lax.broadcasted_iota(jnp.int32, qt.shape, 0)
        zero = jnp.zeros_like(qt)
        q2x.append(jnp.concatenate([jnp.where(chan < DIFF_HEAD_DIM, qt, zero),
                                    jnp.where(chan >= DIFF_HEAD_DIM, qt, zero)], axis=1))

    def step(j, d, first):
        r0 = pl.multiple_of(j * ATT_T, ATT_T)
        logits = []
        for hh in range(ATT_HEADS):
            kt = k_ref[pl.ds(r0, ATT_T), hh * hd:(hh + 1) * hd]
            s = jnp.dot(kt, q2x[hh], preferred_element_type=_F32)
            if d is not None:
                bias = bias_ref[hh, d]
                s = s + jnp.concatenate([bias, bias], axis=1)
            logits.append(s)
        for hh in range(ATT_HEADS):
            s = logits[hh]
            vt = vt_ref[hh * DIFF_V_DIM:(hh + 1) * DIFF_V_DIM, pl.ds(r0, ATT_T)]
            s_max = jnp.max(s, axis=0, keepdims=True)
            if first:
                m_new = s_max
                p = jnp.exp(s - m_new)
                l_scr[hh] = jnp.sum(p, axis=0, keepdims=True)
                acc_scr[hh] = jnp.dot(vt, p.astype(_BF), preferred_element_type=_F32)
            else:
                m_old = m_scr[hh]
                m_new = jnp.maximum(m_old, s_max)
                a = jnp.exp(m_old - m_new)
                p = jnp.exp(s - m_new)
                l_scr[hh] = a * l_scr[hh] + jnp.sum(p, axis=0, keepdims=True)
                acc_scr[hh] = a * acc_scr[hh] + jnp.dot(vt, p.astype(_BF),
                                                        preferred_element_type=_F32)
            m_scr[hh] = m_new

    step(qi, 0, True)

    @pl.when(qi > 0)
    def _():
        step(qi - 1, 1, False)

    def far(j, _):
        step(j, None, False)
        return 0

    lax.fori_loop(0, qi - 1, far, 0)

    lam = (jnp.exp(jnp.sum(lq1_ref[...] * lk1_ref[...], axis=-1, keepdims=True))
           - jnp.exp(jnp.sum(lq2_ref[...] * lk2_ref[...], axis=-1, keepdims=True)) + lam_init)
    for hh in range(ATT_HEADS):
        o = acc_scr[hh] / l_scr[hh]
        o = o[:, :ATT_T] - lam * o[:, ATT_T:]
        o = o * lax.rsqrt(jnp.mean(o * o, axis=0, keepdims=True) + EPS)
        out_ref[:, hh * DIFF_V_DIM:(hh + 1) * DIFF_V_DIM] = o.T * sg_ref[...] * (1.0 - lam_init)


def _diffattn(k3, qt3, vt3, bias_tiles, lq1, lk1, lq2, lk2, subln_g, lam_init):
    B, S, _ = k3.shape
    nq = S // ATT_T
    vec = lambda n: pl.BlockSpec((1, n), lambda b, h, i: (0, 0))
    return pl.pallas_call(
        functools.partial(_diffattn_kernel, lam_init),
        grid=(B, DIFF_HEADS // ATT_HEADS, nq),
        in_specs=[
            pl.BlockSpec((None, ATT_HEADS * 2 * DIFF_HEAD_DIM, ATT_T), lambda b, h, i: (b, h, i)),
            pl.BlockSpec((None, S, ATT_HEADS * 2 * DIFF_HEAD_DIM), lambda b, h, i: (b, 0, h)),
            pl.BlockSpec((None, ATT_HEADS * DIFF_V_DIM, S), lambda b, h, i: (b, h, 0)),
            pl.BlockSpec((ATT_HEADS, 2, ATT_T, ATT_T), lambda b, h, i: (h, 0, 0, 0)),
            vec(DIFF_HEAD_DIM), vec(DIFF_HEAD_DIM), vec(DIFF_HEAD_DIM), vec(DIFF_HEAD_DIM),
            vec(DIFF_V_DIM),
        ],
        out_specs=pl.BlockSpec((None, ATT_T, ATT_HEADS * DIFF_V_DIM), lambda b, h, i: (b, i, h)),
        out_shape=jax.ShapeDtypeStruct((B, S, DIFF_WIDTH), _F32),
        scratch_shapes=[
            pltpu.VMEM((ATT_HEADS, 1, 2 * ATT_T), _F32),
            pltpu.VMEM((ATT_HEADS, 1, 2 * ATT_T), _F32),
            pltpu.VMEM((ATT_HEADS, DIFF_V_DIM, 2 * ATT_T), _F32),
        ],
        compiler_params=pltpu.CompilerParams(
            dimension_semantics=("parallel", "parallel", "arbitrary"),
            vmem_limit_bytes=V7X_VMEM_LIMIT_BYTES),
        name="diffattn",
    )(qt3, k3, vt3, bias_tiles, lq1, lk1, lq2, lk2, subln_g)


def _top16(s, row, payload=None):
    n_rows = s.shape[0]
    groups = range(0, n_rows, SUBLANES)
    vals, rows, pays = [], [], []
    for _ in range(PEER_TOPK):
        v = [s[g:g + SUBLANES] for g in groups]
        extra = [[x[g:g + SUBLANES] for g in groups] for x in ([row] + ([payload] if payload is not None else []))]
        while len(v) > 1:
            keep = [v[a] >= v[a + 1] for a in range(0, len(v) - 1, 2)]
            tail = len(v) % 2
            extra = [[jnp.where(k, x[2 * a], x[2 * a + 1]) for a, k in enumerate(keep)]
                     + ([x[-1]] if tail else []) for x in extra]
            v = ([jnp.maximum(v[2 * a], v[2 * a + 1]) for a in range(len(keep))]
                 + ([v[-1]] if tail else []))
        m = jnp.max(v[0], axis=0, keepdims=True)
        pos = jnp.min(jnp.where(v[0] == m, extra[0][0], n_rows), axis=0, keepdims=True)
        vals.append(m)
        rows.append(pos)
        if payload is not None:
            pays.append(jnp.max(jnp.where(extra[0][0] == pos, extra[1][0], -1), axis=0, keepdims=True))
        s = jnp.where(row == pos, -jnp.inf, s)
    return vals, rows, pays


def _candidates(v1, i1, v2, i2):
    neg = -jnp.inf
    v1c, i1c = jnp.concatenate(v1, axis=0), jnp.concatenate(i1, axis=0)
    v2c, i2c = jnp.concatenate(v2, axis=0), jnp.concatenate(i2, axis=0)
    row = lax.broadcasted_iota(jnp.int32, (SUBLANES, v1c.shape[1]), 0)
    cand = [v1[0] + v2c]
    cidx = [i1[0] * PEER_N_KEYS + i2c]
    for a in range(1, SUBLANES):
        nb = PEER_TOPK // (a + 1)
        cand.append(jnp.where(row < nb, v1[a] + v2c[:SUBLANES], neg))
        cidx.append(i1[a] * PEER_N_KEYS + i2c[:SUBLANES])
    cand.append(v1c[SUBLANES:] + v2[0])
    cidx.append(i1c[SUBLANES:] * PEER_N_KEYS + i2[0])
    return jnp.concatenate(cand, axis=0), jnp.concatenate(cidx, axis=0)


def _route_kernel(x_ref, oa_ref, ob_ref, wo_ref, g_ref, wq_ref, keys_ref, rowid_ref,
                  h1_ref, eid_ref, gate_ref, qp_scr, eid_scr, gate_scr):
    mix = jnp.concatenate([oa_ref[...], ob_ref[...]], axis=-1)
    h1 = x_ref[...] + _mm(mix, wo_ref[...])
    h1_ref[...] = h1
    xn = _rms(h1, g_ref[...])
    qp_scr[...] = _mm(xn, wq_ref[...])
    neg = -jnp.inf

    def head(h, _):
        c0 = pl.multiple_of(h * PEER_DKEY, PEER_DKEY)
        q1 = qp_scr[:, pl.ds(c0, PEER_HALF)]
        q2 = qp_scr[:, pl.ds(c0 + PEER_HALF, PEER_HALF)]
        row = rowid_ref[...]
        v1, i1, _ = _top16(_mm_nt(keys_ref[h, 0], q1), row)
        v2, i2, _ = _top16(_mm_nt(keys_ref[h, 1], q2), row)
        cand, cidx = _candidates(v1, i1, v2, i2)
        best, _, eid = _top16(cand, row[:cand.shape[0]], cidx)
        eid = jnp.concatenate(eid, axis=0)
        best = jnp.concatenate(best, axis=0)
        e = jnp.exp(best - best[0:1])
        gate = e / jnp.sum(e, axis=0, keepdims=True)
        r0 = pl.multiple_of(h * PEER_TOPK, PEER_TOPK)
        eid_scr[pl.ds(r0, PEER_TOPK), :] = eid
        gate_scr[pl.ds(r0, PEER_TOPK), :] = gate
        return 0

    def heads(hg, _):
        for u in range(ROUTE_HEAD_UNROLL):
            head(hg * ROUTE_HEAD_UNROLL + u, 0)
        return 0

    lax.fori_loop(0, PEER_HEADS // ROUTE_HEAD_UNROLL, heads, 0)
    eid_ref[...] = eid_scr[...].T
    gate_ref[...] = gate_scr[...].T


def _route(x2, out_a, out_b, w_out_bf, g_ffn, w_q_bf, keys_bf):
    T = x2.shape[0]
    tm = ROUTE_TM
    full = lambda shape: pl.BlockSpec(shape, lambda i: (0,) * len(shape))
    row_ids = lax.broadcasted_iota(jnp.int32, (PEER_N_KEYS, tm), 0)
    return pl.pallas_call(
        _route_kernel,
        grid=(T // tm,),
        in_specs=[
            pl.BlockSpec((tm, D_MODEL), lambda i: (i, 0)),
            pl.BlockSpec((tm, GMLP_WIDTH), lambda i: (i, 0)),
            pl.BlockSpec((tm, DIFF_WIDTH), lambda i: (i, 0)),
            full((D_MODEL, D_MODEL)),
            full((1, D_MODEL)),
            full((D_MODEL, PEER_HEADS * PEER_DKEY)),
            full((PEER_HEADS, 2, PEER_N_KEYS, PEER_HALF)),
            full((PEER_N_KEYS, tm)),
        ],
        out_specs=[
            pl.BlockSpec((tm, D_MODEL), lambda i: (i, 0)),
            pl.BlockSpec((tm, PEER_SLOTS), lambda i: (i, 0)),
            pl.BlockSpec((tm, PEER_SLOTS), lambda i: (i, 0)),
        ],
        out_shape=[
            jax.ShapeDtypeStruct((T, D_MODEL), _F32),
            jax.ShapeDtypeStruct((T, PEER_SLOTS), jnp.int32),
            jax.ShapeDtypeStruct((T, PEER_SLOTS), _F32),
        ],
        scratch_shapes=[
            pltpu.VMEM((tm, PEER_HEADS * PEER_DKEY), _F32),
            pltpu.VMEM((PEER_SLOTS, tm), jnp.int32),
            pltpu.VMEM((PEER_SLOTS, tm), _F32),
        ],
        compiler_params=pltpu.CompilerParams(
            dimension_semantics=("parallel",), vmem_limit_bytes=V7X_VMEM_LIMIT_BYTES),
        name="route",
    )(x2, out_a, out_b, w_out_bf, g_ffn, w_q_bf, keys_bf, row_ids)


def _peer_kernel(final, eid_ref, gate_ref, h1_ref, g_ref, ut_ref, v_ref, p_ref, gp_ref, wg_ref,
                 wp_ref, gf_ref, h2_ref, w_scr, xn_scr):
    j = pl.program_id(1)

    @pl.when(j == 0)
    def _():
        h1 = h1_ref[...]
        xn_scr[...] = _rms(h1, g_ref[...]).astype(_BF)
        h2_ref[...] = h1
        iota = lax.broadcasted_iota(jnp.int32, (PEER_N_KEYS, PEER_SLOTS), 0)
        zeros = jnp.zeros((PEER_N_KEYS, PEER_SLOTS), _BF)

        def onehots(t):
            e_row = eid_ref[pl.ds(t, 1), :]
            pt = jnp.where(iota == (e_row >> 7), gate_ref[pl.ds(t, 1), :], 0.0).astype(_BF)
            qt = jnp.where(iota == (e_row & (PEER_N_KEYS - 1)), 1.0, 0.0).astype(_BF)
            return pt, qt

        nb = PEER_BUILD_TOKENS

        def build(g0, p):
            oh = [onehots((g0 + b) * SUBLANES + p) for b in range(nb)]
            lhs = jnp.concatenate([pt for pt, _ in oh], axis=1)
            rhs_t = jnp.concatenate(
                [jnp.concatenate([oh[b][1] if c == b else zeros for c in range(nb)], axis=1)
                 for b in range(nb)], axis=0)
            w = lax.dot_general(lhs, rhs_t, _NT, preferred_element_type=_F32)
            for b in range(nb):
                w_scr[g0 + b, pl.ds(p, PEER_N_KEYS, stride=SUBLANES), :] = (
                    w[:, b * PEER_N_KEYS:(b + 1) * PEER_N_KEYS])

        def groups(g, _):
            for u in range(PEER_BUILD_UNROLL):
                for p in range(SUBLANES):
                    build((g * PEER_BUILD_UNROLL + u) * nb, p)
            return 0

        lax.fori_loop(0, PEER_TM // (PEER_BUILD_UNROLL * nb * SUBLANES), groups, 0)

    a = jnp.dot(xn_scr[...], ut_ref[...], preferred_element_type=_F32)
    wa = []
    for il in range(PEER_I1_TILE):
        r0 = pl.multiple_of((j * PEER_I1_TILE + il) * SUBLANES, SUBLANES)
        w = w_scr[:, pl.ds(r0, SUBLANES), :].reshape(PEER_TM, PEER_N_KEYS)
        wa.append((w * jax.nn.gelu(a[:, il * PEER_N_KEYS:(il + 1) * PEER_N_KEYS])).astype(_BF))
    h2_ref[...] += jnp.dot(jnp.concatenate(wa, axis=-1), v_ref[...], preferred_element_type=_F32)

    @pl.when(j == pl.num_programs(1) - 1)
    def _():
        h = h2_ref[...]
        ple_gate = jax.nn.sigmoid(_mm(_rms(h, gp_ref[...]), wg_ref[...]))
        h = h + _mm(p_ref[...], wp_ref[...]) * ple_gate
        h2_ref[...] = _rms(h, gf_ref[...]) if final else h


def _peer_ple(eid, gate, h1, g_ffn, u_t_bf, v_bf, p2, g_ple, w_gate_bf, w_ple_bf, g_final, final):
    T = h1.shape[0]
    n_exp = v_bf.shape[0]
    te = PEER_I1_TILE * PEER_N_KEYS
    once = lambda shape: pl.BlockSpec(shape, lambda i, j: (0,) * len(shape),
                                      pipeline_mode=pl.Buffered(1))
    return pl.pallas_call(
        functools.partial(_peer_kernel, final),
        grid=(T // PEER_TM, n_exp // te),
        in_specs=[
            pl.BlockSpec((PEER_TM, PEER_SLOTS), lambda i, j: (i, 0)),
            pl.BlockSpec((PEER_TM, PEER_SLOTS), lambda i, j: (i, 0)),
            pl.BlockSpec((PEER_TM, D_MODEL), lambda i, j: (i, 0), pipeline_mode=pl.Buffered(1)),
            once((1, D_MODEL)),
            pl.BlockSpec((D_MODEL, te), lambda i, j: (0, j)),
            pl.BlockSpec((te, D_MODEL), lambda i, j: (j, 0)),
            pl.BlockSpec((PEER_TM, PLE_DIM), lambda i, j: (i, 0), pipeline_mode=pl.Buffered(1)),
            once((1, D_MODEL)),
            once((D_MODEL, D_MODEL)),
            once((PLE_DIM, D_MODEL)),
            once((1, D_MODEL)),
        ],
        out_specs=pl.BlockSpec((PEER_TM, D_MODEL), lambda i, j: (i, 0)),
        out_shape=jax.ShapeDtypeStruct((T, D_MODEL), _F32),
        scratch_shapes=[
            pltpu.VMEM((PEER_TM // SUBLANES, PEER_N_KEYS * SUBLANES, PEER_N_KEYS), _F32),
            pltpu.VMEM((PEER_TM, D_MODEL), _BF),
        ],
        compiler_params=pltpu.CompilerParams(
            dimension_semantics=("parallel", "arbitrary"),
            vmem_limit_bytes=V7X_VMEM_LIMIT_BYTES),
        name="peer_ple",
    )(eid, gate, h1, g_ffn, u_t_bf, v_bf, p2, g_ple, w_gate_bf, w_ple_bf, g_final)


def kernel(x, p, g_mix, w_in, gmlp_ln_g, gmlp_ln_b, gmlp_w_s, gmlp_b_s, gmlp_beta, lambda_q1,
           lambda_k1, lambda_q2, lambda_k2, subln_g, rel_bias, w_out, g_ffn, peer_w_q, peer_keys,
           peer_u, peer_v, g_ple, w_ple, w_gate, g_final):
    B, S, D = x.shape
    depth = w_in.shape[0]
    T = B * S
    row = lambda a: a.reshape(1, -1)
    h = x.reshape(T, D)
    bias_tiles = _relbias(rel_bias)
    for i in range(depth):
        lam_init = 0.8 - 0.6 * math.exp(-0.3 * i)
        out_a, k, qt, vt = _mix_in(h, S, row(g_mix[i]), w_in[i].astype(_BF), row(gmlp_ln_g[i]),
                                row(gmlp_ln_b[i]), gmlp_w_s[i], jnp.transpose(gmlp_b_s[i]),
                                row(gmlp_beta[i]))
        out_b = _diffattn(k.reshape(B, S, Q_COLS), qt, vt, bias_tiles, row(lambda_q1[i]),
                          row(lambda_k1[i]), row(lambda_q2[i]), row(lambda_k2[i]),
                          row(subln_g[i]), lam_init)
        h1, eid, gate = _route(h, out_a, out_b.reshape(T, DIFF_WIDTH), w_out[i].astype(_BF),
                               row(g_ffn[i]), peer_w_q[i].astype(_BF), peer_keys[i].astype(_BF))
        h = _peer_ple(eid, gate, h1, row(g_ffn[i]), jnp.transpose(peer_u[i].astype(_BF)),
                      peer_v[i].astype(_BF), p[i].reshape(T, PLE_DIM), row(g_ple[i]),
                      w_gate[i].astype(_BF), w_ple[i].astype(_BF), row(g_final), i == depth - 1)
    return h.reshape(B, S, D)
```

```python
import functools
import math

import jax
import jax.numpy as jnp
from jax import lax
from jax.experimental import pallas as pl
from jax.experimental.pallas import tpu as pltpu

D_MODEL = 1024
CHUNK = 64
GMLP_WIDTH = 512
GMLP_GROUPS = 4
GMLP_GROUP_CH = 128
GMLP_BLOCK = 128
DIFF_HEADS = 4
DIFF_HEAD_DIM = 64
DIFF_V_DIM = 128
DIFF_WIDTH = 512
A_COLS = 1024
Q_COLS = 512
V_COLS = 512
IN_COLS = A_COLS + 2 * Q_COLS + V_COLS
REL_BUCKETS = 32
REL_MAX_EXACT = 8
PEER_N_KEYS = 128
PEER_HEADS = 8
PEER_TOPK = 16
PEER_HALF = 128
PEER_DKEY = 256
PEER_SLOTS = PEER_HEADS * PEER_TOPK
PLE_DIM = 256
EPS = 1e-6
NEG_INF = -1e30
SUBLANES = 8

V7X_VMEM_LIMIT_BYTES = 56 * 1024 * 1024

MIX_TM = 512
ATT_T = 512
ATT_HEADS = 4
RELBIAS_ROWS = 128
ROUTE_TM = 256
ROUTE_HEAD_UNROLL = 4
PEER_TM = 512
PEER_I1_TILE = 8
PEER_BUILD_TOKENS = 2
PEER_BUILD_UNROLL = 8

_NT = (((1,), (1,)), ((), ()))
_BF = jnp.bfloat16
_F32 = jnp.float32


def _rms(x, g):
    return x * lax.rsqrt(jnp.mean(x * x, axis=-1, keepdims=True) + EPS) * g


def _mm(a, b):
    return jnp.dot(a.astype(_BF), b.astype(_BF), preferred_element_type=_F32)


def _mm_nt(a, b):
    return lax.dot_general(a.astype(_BF), b.astype(_BF), _NT, preferred_element_type=_F32)


def _mix_in_kernel(x_ref, g_ref, w_ref, lng_ref, lnb_ref, ws_ref, bst_ref, beta_ref,
                   outa_ref, k_ref, qt_ref, vt_ref):
    n1 = _rms(x_ref[...], g_ref[...])
    z = jnp.dot(n1.astype(_BF), w_ref[...], preferred_element_type=_F32)
    qt_ref[...] = z[:, A_COLS:A_COLS + Q_COLS].T.astype(_BF)
    k_ref[...] = z[:, A_COLS + Q_COLS:A_COLS + 2 * Q_COLS].astype(_BF)
    vt_ref[...] = z[:, A_COLS + 2 * Q_COLS:].T.astype(_BF)
    za = jax.nn.gelu(z[:, :A_COLS])
    pos_i = lax.broadcasted_iota(jnp.int32, (GMLP_BLOCK, GMLP_BLOCK), 0) // CHUNK
    pos_j = lax.broadcasted_iota(jnp.int32, (GMLP_BLOCK, GMLP_BLOCK), 1) // CHUNK
    causal = pos_j <= pos_i
    for g in range(GMLP_GROUPS):
        c0 = g * GMLP_GROUP_CH
        u = za[:, c0:c0 + GMLP_GROUP_CH]
        v = za[:, GMLP_WIDTH + c0:GMLP_WIDTH + c0 + GMLP_GROUP_CH]
        mu = jnp.mean(v, axis=-1, keepdims=True)
        vc = v - mu
        vn = vc * lax.rsqrt(jnp.mean(vc * vc, axis=-1, keepdims=True) + EPS)
        vn = vn * lng_ref[:, c0:c0 + GMLP_GROUP_CH] + lnb_ref[:, c0:c0 + GMLP_GROUP_CH]
        wm = jnp.where(causal, ws_ref[g], 0.0)
        for n in range(MIX_TM // GMLP_BLOCK):
            r0 = n * GMLP_BLOCK
            sv = _mm(wm, vn[r0:r0 + GMLP_BLOCK]) + bst_ref[:, g:g + 1]
            o = u[r0:r0 + GMLP_BLOCK] * sv
            outa_ref[r0:r0 + GMLP_BLOCK, c0:c0 + GMLP_GROUP_CH] = _rms(
                o, beta_ref[:, c0:c0 + GMLP_GROUP_CH])


def _mix_in(x2, seq, g_mix, w_in_bf, ln_g, ln_b, w_s, b_s_t, beta):
    T = x2.shape[0]
    n_seq = seq // MIX_TM
    full = lambda shape: pl.BlockSpec(shape, lambda i: (0,) * len(shape))
    return pl.pallas_call(
        _mix_in_kernel,
        grid=(T // MIX_TM,),
        in_specs=[
            pl.BlockSpec((MIX_TM, D_MODEL), lambda i: (i, 0)),
            full((1, D_MODEL)),
            full((D_MODEL, IN_COLS)),
            full((1, GMLP_WIDTH)),
            full((1, GMLP_WIDTH)),
            full((GMLP_GROUPS, GMLP_BLOCK, GMLP_BLOCK)),
            full((GMLP_BLOCK, GMLP_GROUPS)),
            full((1, GMLP_WIDTH)),
        ],
        out_specs=[
            pl.BlockSpec((MIX_TM, GMLP_WIDTH), lambda i: (i, 0)),
            pl.BlockSpec((MIX_TM, Q_COLS), lambda i: (i, 0)),
            pl.BlockSpec((None, Q_COLS, MIX_TM), lambda i: (i // n_seq, 0, i % n_seq)),
            pl.BlockSpec((None, V_COLS, MIX_TM), lambda i: (i // n_seq, 0, i % n_seq)),
        ],
        out_shape=[
            jax.ShapeDtypeStruct((T, GMLP_WIDTH), _F32),
            jax.ShapeDtypeStruct((T, Q_COLS), _BF),
            jax.ShapeDtypeStruct((T // seq, Q_COLS, seq), _BF),
            jax.ShapeDtypeStruct((T // seq, V_COLS, seq), _BF),
        ],
        compiler_params=pltpu.CompilerParams(
            dimension_semantics=("parallel",), vmem_limit_bytes=V7X_VMEM_LIMIT_BYTES),
        name="mix_in",
    )(x2, g_mix, w_in_bf, ln_g, ln_b, w_s, b_s_t, beta)


def _relbias_kernel(rb_ref, out_ref):
    d, r = pl.program_id(0), pl.program_id(1)
    kj = lax.broadcasted_iota(jnp.int32, (RELBIAS_ROWS, ATT_T), 0) + r * RELBIAS_ROWS
    qi = lax.broadcasted_iota(jnp.int32, (RELBIAS_ROWS, ATT_T), 1)
    rel = kj - qi - d * ATT_T
    n = jnp.abs(rel)
    n2 = n * n
    large = jnp.full_like(n, REL_MAX_EXACT)
    for k in range(1, 8):
        large = large + (n2 >= (1 << (6 + k))).astype(jnp.int32)
    bucket = jnp.where(rel > 0, REL_BUCKETS // 2, 0) + jnp.where(n < REL_MAX_EXACT, n, large)
    visible = ((kj - d * ATT_T) // CHUNK) <= (qi // CHUNK)
    bias = [jnp.zeros(rel.shape, _F32)] * DIFF_HEADS
    for b in range(REL_BUCKETS):
        hit = bucket == b
        bias = [jnp.where(hit, rb_ref[b, h], bias[h]) for h in range(DIFF_HEADS)]
    for h in range(DIFF_HEADS):
        out_ref[h] = jnp.where(visible, bias[h] - rb_ref[REL_BUCKETS // 2 - 1, h], NEG_INF)


def _relbias(rel_bias):
    return pl.pallas_call(
        _relbias_kernel,
        grid=(2, ATT_T // RELBIAS_ROWS),
        in_specs=[pl.BlockSpec(memory_space=pltpu.SMEM)],
        out_specs=pl.BlockSpec((DIFF_HEADS, None, RELBIAS_ROWS, ATT_T), lambda d, r: (0, d, r, 0)),
        out_shape=jax.ShapeDtypeStruct((DIFF_HEADS, 2, ATT_T, ATT_T), _F32),
        name="relbias",
    )(rel_bias)


def _diffattn_kernel(lam_init, qt_ref, k_ref, vt_ref, bias_ref, lq1_ref, lk1_ref, lq2_ref, lk2_ref,
                     sg_ref, out_ref, m_scr, l_scr, acc_scr):
    qi = pl.program_id(2)
    hd = 2 * DIFF_HEAD_DIM
    q2x = []
    for hh in range(ATT_HEADS):
        qt = qt_ref[hh * hd:(hh + 1) * hd, :] * (DIFF_HEAD_DIM ** -0.5)
        chan = lax.broadcasted_iota(jnp.int32, qt.shape, 0)
        zero = jnp.zeros_like(qt)
        q2x.append(jnp.concatenate([jnp.where(chan < DIFF_HEAD_DIM, qt, zero),
                                    jnp.where(chan >= DIFF_HEAD_DIM, qt, zero)], axis=1))

    def step(j, d, first):
        r0 = pl.multiple_of(j * ATT_T, ATT_T)
        logits = []
        for hh in range(ATT_HEADS):
            kt = k_ref[pl.ds(r0, ATT_T), hh * hd:(hh + 1) * hd]
            s = jnp.dot(kt, q2x[hh], preferred_element_type=_F32)
            if d is not None:
                bias = bias_ref[hh, d]
                s = s + jnp.concatenate([bias, bias], axis=1)
            logits.append(s)
        for hh in range(ATT_HEADS):
            s = logits[hh]
            vt = vt_ref[hh * DIFF_V_DIM:(hh + 1) * DIFF_V_DIM, pl.ds(r0, ATT_T)]
            s_max = jnp.max(s, axis=0, keepdims=True)
            if first:
                m_new = s_max
                p = jnp.exp(s - m_new)
                l_scr[hh] = jnp.sum(p, axis=0, keepdims=True)
                acc_scr[hh] = jnp.dot(vt, p.astype(_BF), preferred_element_type=_F32)
            else:
                m_old = m_scr[hh]
                m_new = jnp.maximum(m_old, s_max)
                a = jnp.exp(m_old - m_new)
                p = jnp.exp(s - m_new)
                l_scr[hh] = a * l_scr[hh] + jnp.sum(p, axis=0, keepdims=True)
                acc_scr[hh] = a * acc_scr[hh] + jnp.dot(vt, p.astype(_BF),
                                                        preferred_element_type=_F32)
            m_scr[hh] = m_new

    step(qi, 0, True)

    @pl.when(qi > 0)
    def _():
        step(qi - 1, 1, False)

    def far(j, _):
        step(j, None, False)
        return 0

    lax.fori_loop(0, qi - 1, far, 0)

    lam = (jnp.exp(jnp.sum(lq1_ref[...] * lk1_ref[...], axis=-1, keepdims=True))
           - jnp.exp(jnp.sum(lq2_ref[...] * lk2_ref[...], axis=-1, keepdims=True)) + lam_init)
    for hh in range(ATT_HEADS):
        o = acc_scr[hh] / l_scr[hh]
        o = o[:, :ATT_T] - lam * o[:, ATT_T:]
        o = o * lax.rsqrt(jnp.mean(o * o, axis=0, keepdims=True) + EPS)
        out_ref[:, hh * DIFF_V_DIM:(hh + 1) * DIFF_V_DIM] = o.T * sg_ref[...] * (1.0 - lam_init)


def _diffattn(k3, qt3, vt3, bias_tiles, lq1, lk1, lq2, lk2, subln_g, lam_init):
    B, S, _ = k3.shape
    nq = S // ATT_T
    vec = lambda n: pl.BlockSpec((1, n), lambda b, h, i: (0, 0))
    return pl.pallas_call(
        functools.partial(_diffattn_kernel, lam_init),
        grid=(B, DIFF_HEADS // ATT_HEADS, nq),
        in_specs=[
            pl.BlockSpec((None, ATT_HEADS * 2 * DIFF_HEAD_DIM, ATT_T), lambda b, h, i: (b, h, i)),
            pl.BlockSpec((None, S, ATT_HEADS * 2 * DIFF_HEAD_DIM), lambda b, h, i: (b, 0, h)),
            pl.BlockSpec((None, ATT_HEADS * DIFF_V_DIM, S), lambda b, h, i: (b, h, 0)),
            pl.BlockSpec((ATT_HEADS, 2, ATT_T, ATT_T), lambda b, h, i: (h, 0, 0, 0)),
            vec(DIFF_HEAD_DIM), vec(DIFF_HEAD_DIM), vec(DIFF_HEAD_DIM), vec(DIFF_HEAD_DIM),
            vec(DIFF_V_DIM),
        ],
        out_specs=pl.BlockSpec((None, ATT_T, ATT_HEADS * DIFF_V_DIM), lambda b, h, i: (b, i, h)),
        out_shape=jax.ShapeDtypeStruct((B, S, DIFF_WIDTH), _F32),
        scratch_shapes=[
            pltpu.VMEM((ATT_HEADS, 1, 2 * ATT_T), _F32),
            pltpu.VMEM((ATT_HEADS, 1, 2 * ATT_T), _F32),
            pltpu.VMEM((ATT_HEADS, DIFF_V_DIM, 2 * ATT_T), _F32),
        ],
        compiler_params=pltpu.CompilerParams(
            dimension_semantics=("parallel", "parallel", "arbitrary"),
            vmem_limit_bytes=V7X_VMEM_LIMIT_BYTES),
        name="diffattn",
    )(qt3, k3, vt3, bias_tiles, lq1, lk1, lq2, lk2, subln_g)


def _top16(s, row, payload=None):
    n_rows = s.shape[0]
    groups = range(0, n_rows, SUBLANES)
    vals, rows, pays = [], [], []
    for _ in range(PEER_TOPK):
        v = [s[g:g + SUBLANES] for g in groups]
        extra = [[x[g:g + SUBLANES] for g in groups] for x in ([row] + ([payload] if payload is not None else []))]
        while len(v) > 1:
            keep = [v[a] >= v[a + 1] for a in range(0, len(v) - 1, 2)]
            tail = len(v) % 2
            extra = [[jnp.where(k, x[2 * a], x[2 * a + 1]) for a, k in enumerate(keep)]
                     + ([x[-1]] if tail else []) for x in extra]
            v = ([jnp.maximum(v[2 * a], v[2 * a + 1]) for a in range(len(keep))]
                 + ([v[-1]] if tail else []))
        m = jnp.max(v[0], axis=0, keepdims=True)
        pos = jnp.min(jnp.where(v[0] == m, extra[0][0], n_rows), axis=0, keepdims=True)
        vals.append(m)
        rows.append(pos)
        if payload is not None:
            pays.append(jnp.max(jnp.where(extra[0][0] == pos, extra[1][0], -1), axis=0, keepdims=True))
        s = jnp.where(row == pos, -jnp.inf, s)
    return vals, rows, pays


def _candidates(v1, i1, v2, i2):
    neg = -jnp.inf
    v1c, i1c = jnp.concatenate(v1, axis=0), jnp.concatenate(i1, axis=0)
    v2c, i2c = jnp.concatenate(v2, axis=0), jnp.concatenate(i2, axis=0)
    row = lax.broadcasted_iota(jnp.int32, (SUBLANES, v1c.shape[1]), 0)
    cand = [v1[0] + v2c]
    cidx = [i1[0] * PEER_N_KEYS + i2c]
    for a in range(1, SUBLANES):
        nb = PEER_TOPK // (a + 1)
        cand.append(jnp.where(row < nb, v1[a] + v2c[:SUBLANES], neg))
        cidx.append(i1[a] * PEER_N_KEYS + i2c[:SUBLANES])
    cand.append(v1c[SUBLANES:] + v2[0])
    cidx.append(i1c[SUBLANES:] * PEER_N_KEYS + i2[0])
    return jnp.concatenate(cand, axis=0), jnp.concatenate(cidx, axis=0)


def _route_kernel(x_ref, oa_ref, ob_ref, wo_ref, g_ref, wq_ref, keys_ref, rowid_ref,
                  h1_ref, eid_ref, gate_ref, qp_scr, eid_scr, gate_scr):
    mix = jnp.concatenate([oa_ref[...], ob_ref[...]], axis=-1)
    h1 = x_ref[...] + _mm(mix, wo_ref[...])
    h1_ref[...] = h1
    xn = _rms(h1, g_ref[...])
    qp_scr[...] = _mm(xn, wq_ref[...])
    neg = -jnp.inf

    def head(h, _):
        c0 = pl.multiple_of(h * PEER_DKEY, PEER_DKEY)
        q1 = qp_scr[:, pl.ds(c0, PEER_HALF)]
        q2 = qp_scr[:, pl.ds(c0 + PEER_HALF, PEER_HALF)]
        row = rowid_ref[...]
        v1, i1, _ = _top16(_mm_nt(keys_ref[h, 0], q1), row)
        v2, i2, _ = _top16(_mm_nt(keys_ref[h, 1], q2), row)
        cand, cidx = _candidates(v1, i1, v2, i2)
        best, _, eid = _top16(cand, row[:cand.shape[0]], cidx)
        eid = jnp.concatenate(eid, axis=0)
        best = jnp.concatenate(best, axis=0)
        e = jnp.exp(best - best[0:1])
        gate = e / jnp.sum(e, axis=0, keepdims=True)
        r0 = pl.multiple_of(h * PEER_TOPK, PEER_TOPK)
        eid_scr[pl.ds(r0, PEER_TOPK), :] = eid
        gate_scr[pl.ds(r0, PEER_TOPK), :] = gate
        return 0

    def heads(hg, _):
        for u in range(ROUTE_HEAD_UNROLL):
            head(hg * ROUTE_HEAD_UNROLL + u, 0)
        return 0

    lax.fori_loop(0, PEER_HEADS // ROUTE_HEAD_UNROLL, heads, 0)
    eid_ref[...] = eid_scr[...].T
    gate_ref[...] = gate_scr[...].T


def _route(x2, out_a, out_b, w_out_bf, g_ffn, w_q_bf, keys_bf):
    T = x2.shape[0]
    tm = ROUTE_TM
    full = lambda shape: pl.BlockSpec(shape, lambda i: (0,) * len(shape))
    row_ids = lax.broadcasted_iota(jnp.int32, (PEER_N_KEYS, tm), 0)
    return pl.pallas_call(
        _route_kernel,
        grid=(T // tm,),
        in_specs=[
            pl.BlockSpec((tm, D_MODEL), lambda i: (i, 0)),
            pl.BlockSpec((tm, GMLP_WIDTH), lambda i: (i, 0)),
            pl.BlockSpec((tm, DIFF_WIDTH), lambda i: (i, 0)),
            full((D_MODEL, D_MODEL)),
            full((1, D_MODEL)),
            full((D_MODEL, PEER_HEADS * PEER_DKEY)),
            full((PEER_HEADS, 2, PEER_N_KEYS, PEER_HALF)),
            full((PEER_N_KEYS, tm)),
        ],
        out_specs=[
            pl.BlockSpec((tm, D_MODEL), lambda i: (i, 0)),
            pl.BlockSpec((tm, PEER_SLOTS), lambda i: (i, 0)),
            pl.BlockSpec((tm, PEER_SLOTS), lambda i: (i, 0)),
        ],
        out_shape=[
            jax.ShapeDtypeStruct((T, D_MODEL), _F32),
            jax.ShapeDtypeStruct((T, PEER_SLOTS), jnp.int32),
            jax.ShapeDtypeStruct((T, PEER_SLOTS), _F32),
        ],
        scratch_shapes=[
            pltpu.VMEM((tm, PEER_HEADS * PEER_DKEY), _F32),
            pltpu.VMEM((PEER_SLOTS, tm), jnp.int32),
            pltpu.VMEM((PEER_SLOTS, tm), _F32),
        ],
        compiler_params=pltpu.CompilerParams(
            dimension_semantics=("parallel",), vmem_limit_bytes=V7X_VMEM_LIMIT_BYTES),
        name="route",
    )(x2, out_a, out_b, w_out_bf, g_ffn, w_q_bf, keys_bf, row_ids)


def _peer_kernel(final, eid_ref, gate_ref, h1_ref, g_ref, ut_ref, v_ref, p_ref, gp_ref, wg_ref,
                 wp_ref, gf_ref, h2_ref, w_scr, xn_scr):
    j = pl.program_id(1)

    @pl.when(j == 0)
    def _():
        h1 = h1_ref[...]
        xn_scr[...] = _rms(h1, g_ref[...]).astype(_BF)
        h2_ref[...] = h1
        iota = lax.broadcasted_iota(jnp.int32, (PEER_N_KEYS, PEER_SLOTS), 0)
        zeros = jnp.zeros((PEER_N_KEYS, PEER_SLOTS), _BF)

        def onehots(t):
            e_row = eid_ref[pl.ds(t, 1), :]
            pt = jnp.where(iota == (e_row >> 7), gate_ref[pl.ds(t, 1), :], 0.0).astype(_BF)
            qt = jnp.where(iota == (e_row & (PEER_N_KEYS - 1)), 1.0, 0.0).astype(_BF)
            return pt, qt

        nb = PEER_BUILD_TOKENS

        def build(g0, p):
            oh = [onehots((g0 + b) * SUBLANES + p) for b in range(nb)]
            lhs = jnp.concatenate([pt for pt, _ in oh], axis=1)
            rhs_t = jnp.concatenate(
                [jnp.concatenate([oh[b][1] if c == b else zeros for c in range(nb)], axis=1)
                 for b in range(nb)], axis=0)
            w = lax.dot_general(lhs, rhs_t, _NT, preferred_element_type=_F32)
            for b in range(nb):
                w_scr[g0 + b, pl.ds(p, PEER_N_KEYS, stride=SUBLANES), :] = (
                    w[:, b * PEER_N_KEYS:(b + 1) * PEER_N_KEYS])

        def groups(g, _):
            for u in range(PEER_BUILD_UNROLL):
                for p in range(SUBLANES):
                    build((g * PEER_BUILD_UNROLL + u) * nb, p)
            return 0

        lax.fori_loop(0, PEER_TM // (PEER_BUILD_UNROLL * nb * SUBLANES), groups, 0)

    a = jnp.dot(xn_scr[...], ut_ref[...], preferred_element_type=_F32)
    wa = []
    for il in range(PEER_I1_TILE):
        r0 = pl.multiple_of((j * PEER_I1_TILE + il) * SUBLANES, SUBLANES)
        w = w_scr[:, pl.ds(r0, SUBLANES), :].reshape(PEER_TM, PEER_N_KEYS)
        wa.append((w * jax.nn.gelu(a[:, il * PEER_N_KEYS:(il + 1) * PEER_N_KEYS])).astype(_BF))
    h2_ref[...] += jnp.dot(jnp.concatenate(wa, axis=-1), v_ref[...], preferred_element_type=_F32)

    @pl.when(j == pl.num_programs(1) - 1)
    def _():
        h = h2_ref[...]
        ple_gate = jax.nn.sigmoid(_mm(_rms(h, gp_ref[...]), wg_ref[...]))
        h = h + _mm(p_ref[...], wp_ref[...]) * ple_gate
        h2_ref[...] = _rms(h, gf_ref[...]) if final else h


def _peer_ple(eid, gate, h1, g_ffn, u_t_bf, v_bf, p2, g_ple, w_gate_bf, w_ple_bf, g_final, final):
    T = h1.shape[0]
    n_exp = v_bf.shape[0]
    te = PEER_I1_TILE * PEER_N_KEYS
    once = lambda shape: pl.BlockSpec(shape, lambda i, j: (0,) * len(shape),
                                      pipeline_mode=pl.Buffered(1))
    return pl.pallas_call(
        functools.partial(_peer_kernel, final),
        grid=(T // PEER_TM, n_exp // te),
        in_specs=[
            pl.BlockSpec((PEER_TM, PEER_SLOTS), lambda i, j: (i, 0)),
            pl.BlockSpec((PEER_TM, PEER_SLOTS), lambda i, j: (i, 0)),
            pl.BlockSpec((PEER_TM, D_MODEL), lambda i, j: (i, 0), pipeline_mode=pl.Buffered(1)),
            once((1, D_MODEL)),
            pl.BlockSpec((D_MODEL, te), lambda i, j: (0, j)),
            pl.BlockSpec((te, D_MODEL), lambda i, j: (j, 0)),
            pl.BlockSpec((PEER_TM, PLE_DIM), lambda i, j: (i, 0), pipeline_mode=pl.Buffered(1)),
            once((1, D_MODEL)),
            once((D_MODEL, D_MODEL)),
            once((PLE_DIM, D_MODEL)),
            once((1, D_MODEL)),
        ],
        out_specs=pl.BlockSpec((PEER_TM, D_MODEL), lambda i, j: (i, 0)),
        out_shape=jax.ShapeDtypeStruct((T, D_MODEL), _F32),
        scratch_shapes=[
            pltpu.VMEM((PEER_TM // SUBLANES, PEER_N_KEYS * SUBLANES, PEER_N_KEYS), _F32),
            pltpu.VMEM((PEER_TM, D_MODEL), _BF),
        ],
        compiler_params=pltpu.CompilerParams(
            dimension_semantics=("parallel", "arbitrary"),
            vmem_limit_bytes=V7X_VMEM_LIMIT_BYTES),
        name="peer_ple",
    )(eid, gate, h1, g_ffn, u_t_bf, v_bf, p2, g_ple, w_gate_bf, w_ple_bf, g_final)


def kernel(x, p, g_mix, w_in, gmlp_ln_g, gmlp_ln_b, gmlp_w_s, gmlp_b_s, gmlp_beta, lambda_q1,
           lambda_k1, lambda_q2, lambda_k2, subln_g, rel_bias, w_out, g_ffn, peer_w_q, peer_keys,
           peer_u, peer_v, g_ple, w_ple, w_gate, g_final):
    B, S, D = x.shape
    depth = w_in.shape[0]
    T = B * S
    row = lambda a: a.reshape(1, -1)
    h = x.reshape(T, D)
    bias_tiles = _relbias(rel_bias)
    for i in range(depth):
        lam_init = 0.8 - 0.6 * math.exp(-0.3 * i)
        out_a, k, qt, vt = _mix_in(h, S, row(g_mix[i]), w_in[i].astype(_BF), row(gmlp_ln_g[i]),
                                row(gmlp_ln_b[i]), gmlp_w_s[i], jnp.transpose(gmlp_b_s[i]),
                                row(gmlp_beta[i]))
        out_b = _diffattn(k.reshape(B, S, Q_COLS), qt, vt, bias_tiles, row(lambda_q1[i]),
                          row(lambda_k1[i]), row(lambda_q2[i]), row(lambda_k2[i]),
                          row(subln_g[i]), lam_init)
        h1, eid, gate = _route(h, out_a, out_b.reshape(T, DIFF_WIDTH), w_out[i].astype(_BF),
                               row(g_ffn[i]), peer_w_q[i].astype(_BF), peer_keys[i].astype(_BF))
        h = _peer_ple(eid, gate, h1, row(g_ffn[i]), jnp.transpose(peer_u[i].astype(_BF)),
                      peer_v[i].astype(_BF), p[i].reshape(T, PLE_DIM), row(g_ple[i]),
                      w_gate[i].astype(_BF), w_ple[i].astype(_BF), row(g_final), i == depth - 1)
    return h.reshape(B, S, D)
```

```python
import functools
import math

import jax
import jax.numpy as jnp
from jax import lax
from jax.experimental import pallas as pl
from jax.experimental.pallas import tpu as pltpu

D_MODEL = 1024
CHUNK = 64
GMLP_WIDTH = 512
GMLP_GROUPS = 4
GMLP_GROUP_CH = 128
GMLP_BLOCK = 128
DIFF_HEADS = 4
DIFF_HEAD_DIM = 64
DIFF_V_DIM = 128
DIFF_WIDTH = 512
A_COLS = 1024
Q_COLS = 512
V_COLS = 512
IN_COLS = A_COLS + 2 * Q_COLS + V_COLS
REL_BUCKETS = 32
REL_MAX_EXACT = 8
PEER_N_KEYS = 128
PEER_HEADS = 8
PEER_TOPK = 16
PEER_HALF = 128
PEER_DKEY = 256
PEER_SLOTS = PEER_HEADS * PEER_TOPK
PLE_DIM = 256
EPS = 1e-6
NEG_INF = -1e30
SUBLANES = 8

V7X_VMEM_LIMIT_BYTES = 56 * 1024 * 1024

MIX_TM = 512
ATT_T = 512
ATT_HEADS = 4
RELBIAS_ROWS = 128
ROUTE_TM = 256
ROUTE_HEAD_UNROLL = 4
PEER_TM = 512
PEER_I1_TILE = 8
PEER_BUILD_TOKENS = 2
PEER_BUILD_UNROLL = 8

_NT = (((1,), (1,)), ((), ()))
_BF = jnp.bfloat16
_F32 = jnp.float32


def _rms(x, g):
    return x * lax.rsqrt(jnp.mean(x * x, axis=-1, keepdims=True) + EPS) * g


def _mm(a, b):
    return jnp.dot(a.astype(_BF), b.astype(_BF), preferred_element_type=_F32)


def _mm_nt(a, b):
    return lax.dot_general(a.astype(_BF), b.astype(_BF), _NT, preferred_element_type=_F32)


def _mix_in_kernel(x_ref, g_ref, w_ref, lng_ref, lnb_ref, ws_ref, bst_ref, beta_ref,
                   outa_ref, k_ref, qt_ref, vt_ref):
    n1 = _rms(x_ref[...], g_ref[...])
    z = jnp.dot(n1.astype(_BF), w_ref[...], preferred_element_type=_F32)
    qt_ref[...] = z[:, A_COLS:A_COLS + Q_COLS].T.astype(_BF)
    k_ref[...] = z[:, A_COLS + Q_COLS:A_COLS + 2 * Q_COLS].astype(_BF)
    vt_ref[...] = z[:, A_COLS + 2 * Q_COLS:].T.astype(_BF)
    za = jax.nn.gelu(z[:, :A_COLS])
    pos_i = lax.broadcasted_iota(jnp.int32, (GMLP_BLOCK, GMLP_BLOCK), 0) // CHUNK
    pos_j = lax.broadcasted_iota(jnp.int32, (GMLP_BLOCK, GMLP_BLOCK), 1) // CHUNK
    causal = pos_j <= pos_i
    for g in range(GMLP_GROUPS):
        c0 = g * GMLP_GROUP_CH
        u = za[:, c0:c0 + GMLP_GROUP_CH]
        v = za[:, GMLP_WIDTH + c0:GMLP_WIDTH + c0 + GMLP_GROUP_CH]
        mu = jnp.mean(v, axis=-1, keepdims=True)
        vc = v - mu
        vn = vc * lax.rsqrt(jnp.mean(vc * vc, axis=-1, keepdims=True) + EPS)
        vn = vn * lng_ref[:, c0:c0 + GMLP_GROUP_CH] + lnb_ref[:, c0:c0 + GMLP_GROUP_CH]
        wm = jnp.where(causal, ws_ref[g], 0.0)
        for n in range(MIX_TM // GMLP_BLOCK):
            r0 = n * GMLP_BLOCK
            sv = _mm(wm, vn[r0:r0 + GMLP_BLOCK]) + bst_ref[:, g:g + 1]
            o = u[r0:r0 + GMLP_BLOCK] * sv
            outa_ref[r0:r0 + GMLP_BLOCK, c0:c0 + GMLP_GROUP_CH] = _rms(
                o, beta_ref[:, c0:c0 + GMLP_GROUP_CH])


def _mix_in(x2, seq, g_mix, w_in_bf, ln_g, ln_b, w_s, b_s_t, beta):
    T = x2.shape[0]
    n_seq = seq // MIX_TM
    full = lambda shape: pl.BlockSpec(shape, lambda i: (0,) * len(shape))
    return pl.pallas_call(
        _mix_in_kernel,
        grid=(T // MIX_TM,),
        in_specs=[
            pl.BlockSpec((MIX_TM, D_MODEL), lambda i: (i, 0)),
            full((1, D_MODEL)),
            full((D_MODEL, IN_COLS)),
            full((1, GMLP_WIDTH)),
            full((1, GMLP_WIDTH)),
            full((GMLP_GROUPS, GMLP_BLOCK, GMLP_BLOCK)),
            full((GMLP_BLOCK, GMLP_GROUPS)),
            full((1, GMLP_WIDTH)),
        ],
        out_specs=[
            pl.BlockSpec((MIX_TM, GMLP_WIDTH), lambda i: (i, 0)),
            pl.BlockSpec((MIX_TM, Q_COLS), lambda i: (i, 0)),
            pl.BlockSpec((None, Q_COLS, MIX_TM), lambda i: (i // n_seq, 0, i % n_seq)),
            pl.BlockSpec((None, V_COLS, MIX_TM), lambda i: (i // n_seq, 0, i % n_seq)),
        ],
        out_shape=[
            jax.ShapeDtypeStruct((T, GMLP_WIDTH), _F32),
            jax.ShapeDtypeStruct((T, Q_COLS), _BF),
            jax.ShapeDtypeStruct((T // seq, Q_COLS, seq), _BF),
            jax.ShapeDtypeStruct((T // seq, V_COLS, seq), _BF),
        ],
        compiler_params=pltpu.CompilerParams(
            dimension_semantics=("parallel",), vmem_limit_bytes=V7X_VMEM_LIMIT_BYTES),
        name="mix_in",
    )(x2, g_mix, w_in_bf, ln_g, ln_b, w_s, b_s_t, beta)


def _relbias_kernel(rb_ref, out_ref):
    d, r = pl.program_id(0), pl.program_id(1)
    kj = lax.broadcasted_iota(jnp.int32, (RELBIAS_ROWS, ATT_T), 0) + r * RELBIAS_ROWS
    qi = lax.broadcasted_iota(jnp.int32, (RELBIAS_ROWS, ATT_T), 1)
    rel = kj - qi - d * ATT_T
    n = jnp.abs(rel)
    n2 = n * n
    large = jnp.full_like(n, REL_MAX_EXACT)
    for k in range(1, 8):
        large = large + (n2 >= (1 << (6 + k))).astype(jnp.int32)
    bucket = jnp.where(rel > 0, REL_BUCKETS // 2, 0) + jnp.where(n < REL_MAX_EXACT, n, large)
    visible = ((kj - d * ATT_T) // CHUNK) <= (qi // CHUNK)
    bias = [jnp.zeros(rel.shape, _F32)] * DIFF_HEADS
    for b in range(REL_BUCKETS):
        hit = bucket == b
        bias = [jnp.where(hit, rb_ref[b, h], bias[h]) for h in range(DIFF_HEADS)]
    for h in range(DIFF_HEADS):
        out_ref[h] = jnp.where(visible, bias[h] - rb_ref[REL_BUCKETS // 2 - 1, h], NEG_INF)


def _relbias(rel_bias):
    return pl.pallas_call(
        _relbias_kernel,
        grid=(2, ATT_T // RELBIAS_ROWS),
        in_specs=[pl.BlockSpec(memory_space=pltpu.SMEM)],
        out_specs=pl.BlockSpec((DIFF_HEADS, None, RELBIAS_ROWS, ATT_T), lambda d, r: (0, d, r, 0)),
        out_shape=jax.ShapeDtypeStruct((DIFF_HEADS, 2, ATT_T, ATT_T), _F32),
        name="relbias",
    )(rel_bias)


def _diffattn_kernel(lam_init, qt_ref, k_ref, vt_ref, bias_ref, lq1_ref, lk1_ref, lq2_ref, lk2_ref,
                     sg_ref, out_ref, m_scr, l_scr, acc_scr):
    qi = pl.program_id(2)
    hd = 2 * DIFF_HEAD_DIM
    q2x = []
    for hh in range(ATT_HEADS):
        qt = qt_ref[hh * hd:(hh + 1) * hd, :] * (DIFF_HEAD_DIM ** -0.5)
        chan = lax.broadcasted_iota(jnp.int32, qt.shape, 0)
        zero = jnp.zeros_like(qt)
        q2x.append(jnp.concatenate([jnp.where(chan < DIFF_HEAD_DIM, qt, zero),
                                    jnp.where(chan >= DIFF_HEAD_DIM, qt, zero)], axis=1))

    def step(j, d, first):
        r0 = pl.multiple_of(j * ATT_T, ATT_T)
        logits = []
        for hh in range(ATT_HEADS):
            kt = k_ref[pl.ds(r0, ATT_T), hh * hd:(hh + 1) * hd]
            s = jnp.dot(kt, q2x[hh], preferred_element_type=_F32)
            if d is not None:
                bias = bias_ref[hh, d]
                s = s + jnp.concatenate([bias, bias], axis=1)
            logits.append(s)
        for hh in range(ATT_HEADS):
            s = logits[hh]
            vt = vt_ref[hh * DIFF_V_DIM:(hh + 1) * DIFF_V_DIM, pl.ds(r0, ATT_T)]
            s_max = jnp.max(s, axis=0, keepdims=True)
            if first:
                m_new = s_max
                p = jnp.exp(s - m_new)
                l_scr[hh] = jnp.sum(p, axis=0, keepdims=True)
                acc_scr[hh] = jnp.dot(vt, p.astype(_BF), preferred_element_type=_F32)
            else:
                m_old = m_scr[hh]
                m_new = jnp.maximum(m_old, s_max)
                a = jnp.exp(m_old - m_new)
                p = jnp.exp(s - m_new)
                l_scr[hh] = a * l_scr[hh] + jnp.sum(p, axis=0, keepdims=True)
                acc_scr[hh] = a * acc_scr[hh] + jnp.dot(vt, p.astype(_BF),
                                                        preferred_element_type=_F32)
            m_scr[hh] = m_new

    step(qi, 0, True)

    @pl.when(qi > 0)
    def _():
        step(qi - 1, 1, False)

    def far(j, _):
        step(j, None, False)
        return 0

    lax.fori_loop(0, qi - 1, far, 0)

    lam = (jnp.exp(jnp.sum(lq1_ref[...] * lk1_ref[...], axis=-1, keepdims=True))
           - jnp.exp(jnp.sum(lq2_ref[...] * lk2_ref[...], axis=-1, keepdims=True)) + lam_init)
    for hh in range(ATT_HEADS):
        o = acc_scr[hh] / l_scr[hh]
        o = o[:, :ATT_T] - lam * o[:, ATT_T:]
        o = o * lax.rsqrt(jnp.mean(o * o, axis=0, keepdims=True) + EPS)
        out_ref[:, hh * DIFF_V_DIM:(hh + 1) * DIFF_V_DIM] = o.T * sg_ref[...] * (1.0 - lam_init)


def _diffattn(k3, qt3, vt3, bias_tiles, lq1, lk1, lq2, lk2, subln_g, lam_init):
    B, S, _ = k3.shape
    nq = S // ATT_T
    vec = lambda n: pl.BlockSpec((1, n), lambda b, h, i: (0, 0))
    return pl.pallas_call(
        functools.partial(_diffattn_kernel, lam_init),
        grid=(B, DIFF_HEADS // ATT_HEADS, nq),
        in_specs=[
            pl.BlockSpec((None, ATT_HEADS * 2 * DIFF_HEAD_DIM, ATT_T), lambda b, h, i: (b, h, i)),
            pl.BlockSpec((None, S, ATT_HEADS * 2 * DIFF_HEAD_DIM), lambda b, h, i: (b, 0, h)),
            pl.BlockSpec((None, ATT_HEADS * DIFF_V_DIM, S), lambda b, h, i: (b, h, 0)),
            pl.BlockSpec((ATT_HEADS, 2, ATT_T, ATT_T), lambda b, h, i: (h, 0, 0, 0)),
            vec(DIFF_HEAD_DIM), vec(DIFF_HEAD_DIM), vec(DIFF_HEAD_DIM), vec(DIFF_HEAD_DIM),
            vec(DIFF_V_DIM),
        ],
        out_specs=pl.BlockSpec((None, ATT_T, ATT_HEADS * DIFF_V_DIM), lambda b, h, i: (b, i, h)),
        out_shape=jax.ShapeDtypeStruct((B, S, DIFF_WIDTH), _F32),
        scratch_shapes=[
            pltpu.VMEM((ATT_HEADS, 1, 2 * ATT_T), _F32),
            pltpu.VMEM((ATT_HEADS, 1, 2 * ATT_T), _F32),
            pltpu.VMEM((ATT_HEADS, DIFF_V_DIM, 2 * ATT_T), _F32),
        ],
        compiler_params=pltpu.CompilerParams(
            dimension_semantics=("parallel", "parallel", "arbitrary"),
            vmem_limit_bytes=V7X_VMEM_LIMIT_BYTES),
        name="diffattn",
    )(qt3, k3, vt3, bias_tiles, lq1, lk1, lq2, lk2, subln_g)


def _top16(s, row, payload=None):
    n_rows = s.shape[0]
    groups = range(0, n_rows, SUBLANES)
    vals, rows, pays = [], [], []
    for _ in range(PEER_TOPK):
        v = [s[g:g + SUBLANES] for g in groups]
        extra = [[x[g:g + SUBLANES] for g in groups] for x in ([row] + ([payload] if payload is not None else []))]
        while len(v) > 1:
            keep = [v[a] >= v[a + 1] for a in range(0, len(v) - 1, 2)]
            tail = len(v) % 2
            extra = [[jnp.where(k, x[2 * a], x[2 * a + 1]) for a, k in enumerate(keep)]
                     + ([x[-1]] if tail else []) for x in extra]
            v = ([jnp.maximum(v[2 * a], v[2 * a + 1]) for a in range(len(keep))]
                 + ([v[-1]] if tail else []))
        m = jnp.max(v[0], axis=0, keepdims=True)
        pos = jnp.min(jnp.where(v[0] == m, extra[0][0], n_rows), axis=0, keepdims=True)
        vals.append(m)
        rows.append(pos)
        if payload is not None:
            pays.append(jnp.max(jnp.where(extra[0][0] == pos, extra[1][0], -1), axis=0, keepdims=True))
        s = jnp.where(row == pos, -jnp.inf, s)
    return vals, rows, pays


def _candidates(v1, i1, v2, i2):
    neg = -jnp.inf
    v1c, i1c = jnp.concatenate(v1, axis=0), jnp.concatenate(i1, axis=0)
    v2c, i2c = jnp.concatenate(v2, axis=0), jnp.concatenate(i2, axis=0)
    row = lax.broadcasted_iota(jnp.int32, (SUBLANES, v1c.shape[1]), 0)
    cand = [v1[0] + v2c]
    cidx = [i1[0] * PEER_N_KEYS + i2c]
    for a in range(1, SUBLANES):
        nb = PEER_TOPK // (a + 1)
        cand.append(jnp.where(row < nb, v1[a] + v2c[:SUBLANES], neg))
        cidx.append(i1[a] * PEER_N_KEYS + i2c[:SUBLANES])
    cand.append(v1c[SUBLANES:] + v2[0])
    cidx.append(i1c[SUBLANES:] * PEER_N_KEYS + i2[0])
    return jnp.concatenate(cand, axis=0), jnp.concatenate(cidx, axis=0)


def _route_kernel(x_ref, oa_ref, ob_ref, wo_ref, g_ref, wq_ref, keys_ref, rowid_ref,
                  h1_ref, eid_ref, gate_ref, qp_scr, eid_scr, gate_scr):
    mix = jnp.concatenate([oa_ref[...], ob_ref[...]], axis=-1)
    h1 = x_ref[...] + _mm(mix, wo_ref[...])
    h1_ref[...] = h1
    xn = _rms(h1, g_ref[...])
    qp_scr[...] = _mm(xn, wq_ref[...])
    neg = -jnp.inf

    def head(h, _):
        c0 = pl.multiple_of(h * PEER_DKEY, PEER_DKEY)
        q1 = qp_scr[:, pl.ds(c0, PEER_HALF)]
        q2 = qp_scr[:, pl.ds(c0 + PEER_HALF, PEER_HALF)]
        row = rowid_ref[...]
        v1, i1, _ = _top16(_mm_nt(keys_ref[h, 0], q1), row)
        v2, i2, _ = _top16(_mm_nt(keys_ref[h, 1], q2), row)
        cand, cidx = _candidates(v1, i1, v2, i2)
        best, _, eid = _top16(cand, row[:cand.shape[0]], cidx)
        eid = jnp.concatenate(eid, axis=0)
        best = jnp.concatenate(best, axis=0)
        e = jnp.exp(best - best[0:1])
        gate = e / jnp.sum(e, axis=0, keepdims=True)
        r0 = pl.multiple_of(h * PEER_TOPK, PEER_TOPK)
        eid_scr[pl.ds(r0, PEER_TOPK), :] = eid
        gate_scr[pl.ds(r0, PEER_TOPK), :] = gate
        return 0

    def heads(hg, _):
        for u in range(ROUTE_HEAD_UNROLL):
            head(hg * ROUTE_HEAD_UNROLL + u, 0)
        return 0

    lax.fori_loop(0, PEER_HEADS // ROUTE_HEAD_UNROLL, heads, 0)
    eid_ref[...] = eid_scr[...].T
    gate_ref[...] = gate_scr[...].T


def _route(x2, out_a, out_b, w_out_bf, g_ffn, w_q_bf, keys_bf):
    T = x2.shape[0]
    tm = ROUTE_TM
    full = lambda shape: pl.BlockSpec(shape, lambda i: (0,) * len(shape))
    row_ids = lax.broadcasted_iota(jnp.int32, (PEER_N_KEYS, tm), 0)
    return pl.pallas_call(
        _route_kernel,
        grid=(T // tm,),
        in_specs=[
            pl.BlockSpec((tm, D_MODEL), lambda i: (i, 0)),
            pl.BlockSpec((tm, GMLP_WIDTH), lambda i: (i, 0)),
            pl.BlockSpec((tm, DIFF_WIDTH), lambda i: (i, 0)),
            full((D_MODEL, D_MODEL)),
            full((1, D_MODEL)),
            full((D_MODEL, PEER_HEADS * PEER_DKEY)),
            full((PEER_HEADS, 2, PEER_N_KEYS, PEER_HALF)),
            full((PEER_N_KEYS, tm)),
        ],
        out_specs=[
            pl.BlockSpec((tm, D_MODEL), lambda i: (i, 0)),
            pl.BlockSpec((tm, PEER_SLOTS), lambda i: (i, 0)),
            pl.BlockSpec((tm, PEER_SLOTS), lambda i: (i, 0)),
        ],
        out_shape=[
            jax.ShapeDtypeStruct((T, D_MODEL), _F32),
            jax.ShapeDtypeStruct((T, PEER_SLOTS), jnp.int32),
            jax.ShapeDtypeStruct((T, PEER_SLOTS), _F32),
        ],
        scratch_shapes=[
            pltpu.VMEM((tm, PEER_HEADS * PEER_DKEY), _F32),
            pltpu.VMEM((PEER_SLOTS, tm), jnp.int32),
            pltpu.VMEM((PEER_SLOTS, tm), _F32),
        ],
        compiler_params=pltpu.CompilerParams(
            dimension_semantics=("parallel",), vmem_limit_bytes=V7X_VMEM_LIMIT_BYTES),
        name="route",
    )(x2, out_a, out_b, w_out_bf, g_ffn, w_q_bf, keys_bf, row_ids)


def _peer_kernel(final, eid_ref, gate_ref, h1_ref, g_ref, ut_ref, v_ref, p_ref, gp_ref, wg_ref,
                 wp_ref, gf_ref, h2_ref, w_scr, xn_scr):
    j = pl.program_id(1)

    @pl.when(j == 0)
    def _():
        h1 = h1_ref[...]
        xn_scr[...] = _rms(h1, g_ref[...]).astype(_BF)
        h2_ref[...] = h1
        iota = lax.broadcasted_iota(jnp.int32, (PEER_N_KEYS, PEER_SLOTS), 0)
        zeros = jnp.zeros((PEER_N_KEYS, PEER_SLOTS), _BF)

        def onehots(t):
            e_row = eid_ref[pl.ds(t, 1), :]
            pt = jnp.where(iota == (e_row >> 7), gate_ref[pl.ds(t, 1), :], 0.0).astype(_BF)
            qt = jnp.where(iota == (e_row & (PEER_N_KEYS - 1)), 1.0, 0.0).astype(_BF)
            return pt, qt

        nb = PEER_BUILD_TOKENS

        def build(g0, p):
            oh = [onehots((g0 + b) * SUBLANES + p) for b in range(nb)]
            lhs = jnp.concatenate([pt for pt, _ in oh], axis=1)
            rhs_t = jnp.concatenate(
                [jnp.concatenate([oh[b][1] if c == b else zeros for c in range(nb)], axis=1)
                 for b in range(nb)], axis=0)
            w = lax.dot_general(lhs, rhs_t, _NT, preferred_element_type=_F32)
            for b in range(nb):
                w_scr[g0 + b, pl.ds(p, PEER_N_KEYS, stride=SUBLANES), :] = (
                    w[:, b * PEER_N_KEYS:(b + 1) * PEER_N_KEYS])

        def groups(g, _):
            for u in range(PEER_BUILD_UNROLL):
                for p in range(SUBLANES):
                    build((g * PEER_BUILD_UNROLL + u) * nb, p)
            return 0

        lax.fori_loop(0, PEER_TM // (PEER_BUILD_UNROLL * nb * SUBLANES), groups, 0)

    a = jnp.dot(xn_scr[...], ut_ref[...], preferred_element_type=_F32)
    wa = []
    for il in range(PEER_I1_TILE):
        r0 = pl.multiple_of((j * PEER_I1_TILE + il) * SUBLANES, SUBLANES)
        w = w_scr[:, pl.ds(r0, SUBLANES), :].reshape(PEER_TM, PEER_N_KEYS)
        wa.append((w * jax.nn.gelu(a[:, il * PEER_N_KEYS:(il + 1) * PEER_N_KEYS])).astype(_BF))
    h2_ref[...] += jnp.dot(jnp.concatenate(wa, axis=-1), v_ref[...], preferred_element_type=_F32)

    @pl.when(j == pl.num_programs(1) - 1)
    def _():
        h = h2_ref[...]
        ple_gate = jax.nn.sigmoid(_mm(_rms(h, gp_ref[...]), wg_ref[...]))
        h = h + _mm(p_ref[...], wp_ref[...]) * ple_gate
        h2_ref[...] = _rms(h, gf_ref[...]) if final else h


def _tables_kernel(u_ref, v_ref, ut_ref, vb_ref):
    ut_ref[...] = u_ref[...].T.astype(_BF)
    vb_ref[...] = v_ref[...].astype(_BF)


def _tables(peer_u, peer_v):
    n_exp, d = peer_u.shape
    te = PEER_I1_TILE * PEER_N_KEYS
    return pl.pallas_call(
        _tables_kernel,
        grid=(n_exp // te,),
        in_specs=[pl.BlockSpec((te, d), lambda j: (j, 0)), pl.BlockSpec((te, d), lambda j: (j, 0))],
        out_specs=[pl.BlockSpec((d, te), lambda j: (0, j)), pl.BlockSpec((te, d), lambda j: (j, 0))],
        out_shape=[jax.ShapeDtypeStruct((d, n_exp), _BF), jax.ShapeDtypeStruct((n_exp, d), _BF)],
        compiler_params=pltpu.CompilerParams(
            dimension_semantics=("parallel",), vmem_limit_bytes=V7X_VMEM_LIMIT_BYTES),
        name="tables",
    )(peer_u, peer_v)


def _peer_ple(eid, gate, h1, g_ffn, u_t_bf, v_bf, p2, g_ple, w_gate_bf, w_ple_bf, g_final, final):
    T = h1.shape[0]
    n_exp = v_bf.shape[0]
    te = PEER_I1_TILE * PEER_N_KEYS
    once = lambda shape: pl.BlockSpec(shape, lambda i, j: (0,) * len(shape),
                                      pipeline_mode=pl.Buffered(1))
    return pl.pallas_call(
        functools.partial(_peer_kernel, final),
        grid=(T // PEER_TM, n_exp // te),
        in_specs=[
            pl.BlockSpec((PEER_TM, PEER_SLOTS), lambda i, j: (i, 0)),
            pl.BlockSpec((PEER_TM, PEER_SLOTS), lambda i, j: (i, 0)),
            pl.BlockSpec((PEER_TM, D_MODEL), lambda i, j: (i, 0), pipeline_mode=pl.Buffered(1)),
            once((1, D_MODEL)),
            pl.BlockSpec((D_MODEL, te), lambda i, j: (0, j)),
            pl.BlockSpec((te, D_MODEL), lambda i, j: (j, 0)),
            pl.BlockSpec((PEER_TM, PLE_DIM), lambda i, j: (i, 0), pipeline_mode=pl.Buffered(1)),
            once((1, D_MODEL)),
            once((D_MODEL, D_MODEL)),
            once((PLE_DIM, D_MODEL)),
            once((1, D_MODEL)),
        ],
        out_specs=pl.BlockSpec((PEER_TM, D_MODEL), lambda i, j: (i, 0)),
        out_shape=jax.ShapeDtypeStruct((T, D_MODEL), _F32),
        scratch_shapes=[
            pltpu.VMEM((PEER_TM // SUBLANES, PEER_N_KEYS * SUBLANES, PEER_N_KEYS), _F32),
            pltpu.VMEM((PEER_TM, D_MODEL), _BF),
        ],
        compiler_params=pltpu.CompilerParams(
            dimension_semantics=("parallel", "arbitrary"),
            vmem_limit_bytes=V7X_VMEM_LIMIT_BYTES),
        name="peer_ple",
    )(eid, gate, h1, g_ffn, u_t_bf, v_bf, p2, g_ple, w_gate_bf, w_ple_bf, g_final)


def kernel(x, p, g_mix, w_in, gmlp_ln_g, gmlp_ln_b, gmlp_w_s, gmlp_b_s, gmlp_beta, lambda_q1,
           lambda_k1, lambda_q2, lambda_k2, subln_g, rel_bias, w_out, g_ffn, peer_w_q, peer_keys,
           peer_u, peer_v, g_ple, w_ple, w_gate, g_final):
    B, S, D = x.shape
    depth = w_in.shape[0]
    T = B * S
    row = lambda a: a.reshape(1, -1)
    h = x.reshape(T, D)
    bias_tiles = _relbias(rel_bias)
    for i in range(depth):
        lam_init = 0.8 - 0.6 * math.exp(-0.3 * i)
        out_a, k, qt, vt = _mix_in(h, S, row(g_mix[i]), w_in[i].astype(_BF), row(gmlp_ln_g[i]),
                                row(gmlp_ln_b[i]), gmlp_w_s[i], jnp.transpose(gmlp_b_s[i]),
                                row(gmlp_beta[i]))
        out_b = _diffattn(k.reshape(B, S, Q_COLS), qt, vt, bias_tiles, row(lambda_q1[i]),
                          row(lambda_k1[i]), row(lambda_q2[i]), row(lambda_k2[i]),
                          row(subln_g[i]), lam_init)
        h1, eid, gate = _route(h, out_a, out_b.reshape(T, DIFF_WIDTH), w_out[i].astype(_BF),
                               row(g_ffn[i]), peer_w_q[i].astype(_BF), peer_keys[i].astype(_BF))
        u_t_bf, v_bf = _tables(peer_u[i], peer_v[i])
        h = _peer_ple(eid, gate, h1, row(g_ffn[i]), u_t_bf, v_bf,
                      p[i].reshape(T, PLE_DIM), row(g_ple[i]),
                      w_gate[i].astype(_BF), w_ple[i].astype(_BF), row(g_final), i == depth - 1)
    return h.reshape(B, S, D)
```
